```python
import math
import functools
import jax
import jax.numpy as jnp
from jax import lax
import numpy as np

D_MODEL = 2048
BATCH = 4
SEQ = 2048
DEPTH = 4
DEC_BATCH = 128
DEC_SEQ = 4
PAST_LEN = 16384
PAGE_SIZE = 128

N_META = 16
CHUNK = 128
CONV_W = 4
D_MIX = 2 * D_MODEL
MLSTM_W = D_MIX // 4
MLSTM_HEADS = 4
MLSTM_DH = MLSTM_W // MLSTM_HEADS
SSD_W = D_MIX // 2
SSD_HEADDIM = 64
SSD_HEADS = SSD_W // SSD_HEADDIM
SSD_STATE = 128
SSD_GROUPS = 2
SSD_CONV_DIM = SSD_W + 2 * SSD_GROUPS * SSD_STATE
LRU_W = D_MIX // 4
LRU_BLOCKS = 8
LRU_BW = LRU_W // LRU_BLOCKS
LRU_C = 8.0
N_EXPERTS = 32
TOP_K = 4
D_EXPERT = D_MODEL
SWIGLU_LIMIT = 7.0
SWIGLU_ALPHA = 1.702
MOE_BLOCK = 128
DN_ALPHA = (2 * DEPTH) ** 0.25
DN_BETA = (8 * DEPTH) ** -0.25
LN_EPS = 1e-5
RMS_EPS = 1e-6

IN_DIM = 4 * MLSTM_W + 2 * MLSTM_HEADS + SSD_W + SSD_CONV_DIM + SSD_HEADS + 2 * LRU_W

kernel_name = "hymba_mlstm_ssd_rglru_moe_step"


def _in_split_points():
    sizes = [MLSTM_W, MLSTM_W, MLSTM_W, MLSTM_W, MLSTM_HEADS, MLSTM_HEADS,
             SSD_W, SSD_CONV_DIM, SSD_HEADS, LRU_W, LRU_W]
    return [int(s) for s in np.cumsum(sizes)[:-1]]


def layer_norm(x, g, b):
    xf = x.astype(jnp.float32)
    mu = jnp.mean(xf, -1, keepdims=True)
    var = jnp.mean(jnp.square(xf - mu), -1, keepdims=True)
    return ((xf - mu) * lax.rsqrt(var + LN_EPS) * g + b).astype(x.dtype)


def rms_norm(x, g):
    xf = x.astype(jnp.float32)
    return xf * lax.rsqrt(jnp.mean(jnp.square(xf), -1, keepdims=True) + RMS_EPS) * g


def causal_conv(u, buf, w, b):
    full = jnp.concatenate([buf.astype(u.dtype), u], axis=1)
    out = lax.conv_general_dilated(full, w.astype(u.dtype)[:, None, :], (1,), 'VALID',
                                   dimension_numbers=('NWC', 'WIO', 'NWC'),
                                   feature_group_count=u.shape[-1])
    return out + b.astype(u.dtype), full[:, -(CONV_W - 1):]


def _to_chunks(a, c):
    bsz, L = a.shape[:2]
    return jnp.moveaxis(a.reshape(bsz, L // c, c, *a.shape[2:]), 1, 0)


def _from_chunks(a):
    nc, bsz, c = a.shape[:3]
    return jnp.moveaxis(a, 0, 1).reshape(bsz, nc * c, *a.shape[3:])


def mlstm_chunked(q, k, v, i_log, f_log, c0, n0, m0):
    L = q.shape[1]
    c = math.gcd(L, CHUNK)
    causal = jnp.tril(jnp.ones((c, c), bool))[None, :, :, None]

    def step(carry, inp):
        cm, nm, mm = carry
        qc, kc, vc, ic, fc = inp
        b = jnp.cumsum(fc, axis=1)
        log_d = jnp.where(causal, b[:, :, None] - b[:, None] + ic[:, None], -jnp.inf)
        log_p = b + mm[:, None]
        m_t = jnp.maximum(log_p, jnp.max(log_d, axis=2))
        d = jnp.exp(log_d - m_t[:, :, None])
        wp = jnp.exp(log_p - m_t)
        s = jnp.einsum('bthd,bshd->btsh', qc, kc) * d
        num = jnp.einsum('btsh,bshe->bthe', s, vc) + wp[..., None] * jnp.einsum('bthd,bhde->bthe', qc, cm)
        den = jnp.sum(s, axis=2) + wp * jnp.einsum('bthd,bhd->bth', qc, nm)
        h = num / jnp.maximum(jnp.abs(den), jnp.exp(-m_t))[..., None]
        b_end = b[:, -1]
        m_new = m_t[:, -1]
        w_s = jnp.exp(b_end[:, None] - b + ic - m_new[:, None])
        w_prev = jnp.exp(b_end + mm - m_new)
        c_new = w_prev[..., None, None] * cm + jnp.einsum('bsh,bshd,bshe->bhde', w_s, kc, vc)
        n_new = w_prev[..., None] * nm + jnp.einsum('bsh,bshd->bhd', w_s, kc)
        return (c_new, n_new, m_new), h

    (cm, nm, mm), hs = lax.scan(step, (c0, n0, m0), tuple(_to_chunks(t, c) for t in (q, k, v, i_log, f_log)))
    return _from_chunks(hs), cm, nm, mm


def ssd_chunked(x, dt, bm, cm, h0, a):
    bsz, L, H, P = x.shape
    G, N = bm.shape[2], bm.shape[3]
    hg = H // G
    c = math.gcd(L, CHUNK)
    causal = jnp.tril(jnp.ones((c, c), bool))[None, :, :, None]

    def step(h, inp):
        xc, dtc, bc, cc = inp
        acum = jnp.cumsum(dtc * a, axis=1)
        seg = jnp.exp(jnp.where(causal, acum[:, :, None] - acum[:, None], -jnp.inf))
        cb = jnp.repeat(jnp.einsum('btgn,bsgn->btsg', cc, bc), hg, axis=-1)
        xdt = xc * dtc[..., None]
        y = jnp.einsum('btsh,bshp->bthp', cb * seg, xdt)
        y = y + jnp.einsum('btgn,bgjpn->btgjp', cc, h.reshape(bsz, G, hg, P, N)).reshape(bsz, c, H, P) * jnp.exp(acum)[..., None]
        xw = (xdt * jnp.exp(acum[:, -1:] - acum)[..., None]).reshape(bsz, c, G, hg, P)
        h = jnp.exp(acum[:, -1])[..., None, None] * h + jnp.einsum('bsgjp,bsgn->bgjpn', xw, bc).reshape(bsz, H, P, N)
        return h, y

    h, ys = lax.scan(step, h0, tuple(_to_chunks(t, c) for t in (x, dt, bm, cm)))
    return _from_chunks(ys), h


def rglru(x, r, i, lam, h0):
    log_a = -LRU_C * r * jax.nn.softplus(-lam)
    a = jnp.exp(log_a)
    b = jnp.sqrt(-jnp.expm1(2.0 * log_a)) * (i * x)

    def combine(lhs, rhs):
        a1, b1 = lhs
        a2, b2 = rhs
        return a1 * a2, a2 * b1 + b2

    a_cum, b_cum = lax.associative_scan(combine, (a, b), axis=1)
    h = a_cum * h0[:, None, :] + b_cum
    return h, h[:, -1]


def _segments(fn, seqs, state, n_lead):
    if n_lead == 0:
        return fn(*seqs, *state)
    y0, *state = fn(*(s[:, :n_lead] for s in seqs), *state)
    y1, *state = fn(*(s[:, n_lead:] for s in seqs), *state)
    return (jnp.concatenate([y0, y1], axis=1), *state)


def mixer(u, state, w, n_lead):
    (w_in, b_if, conv_ssd_w, conv_ssd_b, dt_bias, a_log, d_skip, g_ssd, g_mlstm,
     conv_lru_w, conv_lru_b, w_rg_a, b_rg_a, w_rg_i, b_rg_i, lam, w_out) = w
    c_m, n_m, m_m, h_ssd, buf_ssd, h_lru, buf_lru = state
    f32 = jnp.float32
    bsz, L, _ = u.shape
    proj = jnp.einsum('bld,de->ble', u, w_in)
    (q, k, v, o_pre, i_pre, f_pre, z, xbc, dt_pre, x_lru, y_lru) = jnp.split(proj, _in_split_points(), axis=-1)

    def heads(t):
        return t.astype(f32).reshape(bsz, L, MLSTM_HEADS, MLSTM_DH)
    b_i, b_f = jnp.split(b_if.astype(f32), 2)
    i_log = i_pre.astype(f32) + b_i
    f_log = jax.nn.log_sigmoid(f_pre.astype(f32) + b_f)
    h_m, c_m, n_m, m_m = _segments(
        mlstm_chunked, (heads(q), heads(k) * MLSTM_DH ** -0.5, heads(v), i_log, f_log),
        (c_m.astype(f32), n_m.astype(f32), m_m.astype(f32)), n_lead)
    h_m = rms_norm(h_m, g_mlstm.reshape(MLSTM_HEADS, MLSTM_DH)) * jax.nn.sigmoid(heads(o_pre))
    out_m = h_m.reshape(bsz, L, MLSTM_W)

    xbc, buf_ssd = causal_conv(xbc, buf_ssd, conv_ssd_w, conv_ssd_b)
    xbc = jax.nn.silu(xbc.astype(f32))
    x_s, b_s, c_s = jnp.split(xbc, [SSD_W, SSD_W + SSD_GROUPS * SSD_STATE], axis=-1)
    x_s = x_s.reshape(bsz, L, SSD_HEADS, SSD_HEADDIM)
    b_s = b_s.reshape(bsz, L, SSD_GROUPS, SSD_STATE)
    c_s = c_s.reshape(bsz, L, SSD_GROUPS, SSD_STATE)
    dt = jax.nn.softplus(dt_pre.astype(f32) + dt_bias)
    a = -jnp.exp(a_log.astype(f32))
    y_s, h_ssd = _segments(functools.partial(ssd_chunked, a=a), (x_s, dt, b_s, c_s), (h_ssd.astype(f32),), n_lead)
    y_s = (y_s + d_skip[:, None] * x_s).reshape(bsz, L, SSD_W) * jax.nn.silu(z.astype(f32))
    out_s = rms_norm(y_s, g_ssd)

    x_l, buf_lru = causal_conv(x_lru, buf_lru, conv_lru_w, conv_lru_b)
    x_l = x_l.astype(f32)
    xb = x_l.reshape(bsz, L, LRU_BLOCKS, LRU_BW)
    r = jax.nn.sigmoid(jnp.einsum('blni,nij->blnj', xb, w_rg_a).reshape(bsz, L, LRU_W) + b_rg_a)
    ig = jax.nn.sigmoid(jnp.einsum('blni,nij->blnj', xb, w_rg_i).reshape(bsz, L, LRU_W) + b_rg_i)
    h_seq, h_lru = rglru(x_l, r, ig, lam.astype(f32), h_lru.astype(f32))
    out_l = h_seq * jax.nn.gelu(y_lru.astype(f32))

    mix = jnp.concatenate([out_m, out_s, out_l], axis=-1).astype(u.dtype)
    y = jnp.einsum('ble,ed->bld', mix, w_out)
    return y, (c_m, n_m, m_m, h_ssd, buf_ssd, h_lru, buf_lru)


def moe(h, w_router, b_router, w_gate, b_gate, w_up, b_up, w_down, b_down):
    T, D = h.shape
    logits = jnp.dot(h, w_router).astype(jnp.float32) + b_router
    top_val, top_idx = lax.top_k(logits, TOP_K)
    gates = jax.nn.softmax(top_val, axis=-1)
    n_assign = T * TOP_K
    flat_e = top_idx.reshape(-1)
    order = jnp.argsort(flat_e)
    sorted_e = flat_e[order]
    counts = jnp.bincount(flat_e, length=N_EXPERTS)
    padded = (counts + MOE_BLOCK - 1) // MOE_BLOCK * MOE_BLOCK
    pad_end = jnp.cumsum(padded)
    pad_start = pad_end - padded
    start = jnp.cumsum(counts) - counts
    dest_sorted = pad_start[sorted_e] + jnp.arange(n_assign) - start[sorted_e]
    n_blocks = -(-(n_assign + N_EXPERTS * (MOE_BLOCK - 1)) // MOE_BLOCK)
    n_rows = n_blocks * MOE_BLOCK
    row_tok = jnp.full((n_rows,), T, jnp.int32).at[dest_sorted].set((order // TOP_K).astype(jnp.int32))
    blk_e = jnp.minimum(jnp.searchsorted(pad_end, jnp.arange(n_blocks) * MOE_BLOCK, side='right'), N_EXPERTS - 1)
    h_pad = jnp.concatenate([h, jnp.zeros((1, D), h.dtype)], axis=0)
    xb = h_pad[row_tok].reshape(n_blocks, MOE_BLOCK, D)

    def expert_block(args):
        xblk, e = args
        gate = jnp.minimum(jnp.dot(xblk, w_gate[e]) + b_gate[e], SWIGLU_LIMIT)
        up = jnp.clip(jnp.dot(xblk, w_up[e]) + b_up[e], -SWIGLU_LIMIT, SWIGLU_LIMIT)
        act = (up + 1.0) * gate * jax.nn.sigmoid(SWIGLU_ALPHA * gate)
        return jnp.dot(act, w_down[e]) + b_down[e]

    yb = lax.map(expert_block, (xb, blk_e)).reshape(n_rows, D)
    dest = jnp.zeros((n_assign,), jnp.int32).at[order].set(dest_sorted.astype(jnp.int32))
    y = yb[dest].reshape(T, TOP_K, D)
    return jnp.einsum('tk,tkd->td', gates.astype(y.dtype), y)


def trunk(x, state, mix_w, moe_w, norms, n_lead, out_dtypes):
    ln1_g, ln1_b, ln2_g, ln2_b = norms
    new = [[] for _ in state]
    for l in range(DEPTH):
        st = [s[l] for s in state]
        y, st_new = mixer(x, st, [p[l] for p in mix_w], n_lead)
        x = layer_norm(DN_ALPHA * x + y, ln1_g[l], ln1_b[l])
        bsz, L, d = x.shape
        f = moe(x.reshape(bsz * L, d), *[p[l] for p in moe_w]).reshape(bsz, L, d)
        x = layer_norm(DN_ALPHA * x + f, ln2_g[l], ln2_b[l])
        for lst, s in zip(new, st_new):
            lst.append(s)
    return x, [jnp.stack(lst).astype(dt) for lst, dt in zip(new, out_dtypes)]


def setup_inputs(seed: int = 0) -> dict:
    key = jax.random.key(seed)
    ks = iter(jax.random.split(key, 64))
    f32 = jnp.float32

    def nrm(shape, scale):
        return jax.random.normal(next(ks), shape, f32) * scale

    def unif(shape, lo, hi):
        return jax.random.uniform(next(ks), shape, f32, lo, hi)

    D, E, F = D_MODEL, N_EXPERTS, D_EXPERT
    dt0 = jnp.exp(unif((DEPTH, SSD_HEADS), math.log(1e-3), math.log(1e-1)))
    a_base = unif((DEPTH, LRU_W), 0.9, 0.999) ** (1.0 / LRU_C)
    b_if = jnp.concatenate([nrm((DEPTH, MLSTM_HEADS), 0.1),
                            jnp.linspace(3.0, 6.0, MLSTM_HEADS)[None] + nrm((DEPTH, MLSTM_HEADS), 0.1)], axis=-1)
    return {
        "x_prompt": nrm((BATCH, SEQ, D), 1.0),
        "x_sample": nrm((DEC_BATCH, DEC_SEQ, D), 1.0),
        "state_mlstm_C": nrm((DEPTH, DEC_BATCH, MLSTM_HEADS, MLSTM_DH, MLSTM_DH), 0.5),
        "state_mlstm_n": nrm((DEPTH, DEC_BATCH, MLSTM_HEADS, MLSTM_DH), 0.5),
        "state_mlstm_m": nrm((DEPTH, DEC_BATCH, MLSTM_HEADS), 0.5),
        "state_ssd_h": nrm((DEPTH, DEC_BATCH, SSD_HEADS, SSD_HEADDIM, SSD_STATE), 0.5),
        "state_ssd_conv": nrm((DEPTH, DEC_BATCH, CONV_W - 1, SSD_CONV_DIM), 1.0),
        "state_lru_h": nrm((DEPTH, DEC_BATCH, LRU_W), 0.5),
        "state_lru_conv": nrm((DEPTH, DEC_BATCH, CONV_W - 1, LRU_W), 1.0),
        "meta": nrm((N_META, D), 1.0),
        "ln_in_g": 1.0 + nrm((D,), 0.02),
        "ln_in_b": nrm((D,), 0.02),
        "w_in": nrm((DEPTH, D, IN_DIM), D ** -0.5),
        "b_if": b_if,
        "conv_ssd_w": nrm((DEPTH, CONV_W, SSD_CONV_DIM), CONV_W ** -0.5),
        "conv_ssd_b": nrm((DEPTH, SSD_CONV_DIM), 0.02),
        "dt_bias": dt0 + jnp.log(-jnp.expm1(-dt0)),
        "a_log": jnp.log(unif((DEPTH, SSD_HEADS), 1.0, 16.0)),
        "d_skip": 1.0 + nrm((DEPTH, SSD_HEADS), 0.1),
        "g_ssd": 1.0 + nrm((DEPTH, SSD_W), 0.02),
        "g_mlstm": 1.0 + nrm((DEPTH, MLSTM_W), 0.02),
        "conv_lru_w": nrm((DEPTH, CONV_W, LRU_W), CONV_W ** -0.5),
        "conv_lru_b": nrm((DEPTH, LRU_W), 0.02),
        "w_rg_a": nrm((DEPTH, LRU_BLOCKS, LRU_BW, LRU_BW), LRU_BW ** -0.5),
        "b_rg_a": nrm((DEPTH, LRU_W), 0.02),
        "w_rg_i": nrm((DEPTH, LRU_BLOCKS, LRU_BW, LRU_BW), LRU_BW ** -0.5),
        "b_rg_i": nrm((DEPTH, LRU_W), 0.02),
        "lam": jnp.log(a_base) - jnp.log1p(-a_base),
        "w_out": nrm((DEPTH, D_MIX, D), DN_BETA * (2.0 / (D_MIX + D)) ** 0.5),
        "ln1_g": 1.0 + nrm((DEPTH, D), 0.02),
        "ln1_b": nrm((DEPTH, D), 0.02),
        "w_router": nrm((DEPTH, D, E), D ** -0.5),
        "b_router": nrm((DEPTH, E), 0.01),
        "w_gate": nrm((DEPTH, E, D, F), D ** -0.5),
        "b_gate": nrm((DEPTH, E, F), 0.02),
        "w_up": nrm((DEPTH, E, D, F), D ** -0.5),
        "b_up": nrm((DEPTH, E, F), 0.02),
        "w_down": nrm((DEPTH, E, F, D), DN_BETA * (2.0 / (F + D)) ** 0.5),
        "b_down": nrm((DEPTH, E, D), 0.02),
        "ln2_g": 1.0 + nrm((DEPTH, D), 0.02),
        "ln2_b": nrm((DEPTH, D), 0.02),
    }


def reference(x_prompt, x_sample, state_mlstm_C, state_mlstm_n, state_mlstm_m, state_ssd_h, state_ssd_conv,
              state_lru_h, state_lru_conv, meta, ln_in_g, ln_in_b, w_in, b_if, conv_ssd_w, conv_ssd_b, dt_bias,
              a_log, d_skip, g_ssd, g_mlstm, conv_lru_w, conv_lru_b, w_rg_a, b_rg_a, w_rg_i, b_rg_i, lam, w_out,
              ln1_g, ln1_b, w_router, b_router, w_gate, b_gate, w_up, b_up, w_down, b_down, ln2_g, ln2_b):
    f32 = jnp.float32
    mix_w = (w_in, b_if, conv_ssd_w, conv_ssd_b, dt_bias, a_log, d_skip, g_ssd, g_mlstm,
             conv_lru_w, conv_lru_b, w_rg_a, b_rg_a, w_rg_i, b_rg_i, lam, w_out)
    moe_w = (w_router, b_router, w_gate, b_gate, w_up, b_up, w_down, b_down)
    norms = (ln1_g, ln1_b, ln2_g, ln2_b)
    sample_state = (state_mlstm_C, state_mlstm_n, state_mlstm_m, state_ssd_h, state_ssd_conv,
                    state_lru_h, state_lru_conv)
    dtypes = [s.dtype for s in sample_state]

    bp = x_prompt.shape[0]
    meta_b = jnp.broadcast_to(meta.astype(x_prompt.dtype)[None], (bp, N_META, meta.shape[-1]))
    xp = layer_norm(jnp.concatenate([meta_b, x_prompt], axis=1), ln_in_g, ln_in_b)
    zero_state = (jnp.zeros((DEPTH, bp, MLSTM_HEADS, MLSTM_DH, MLSTM_DH), f32),
                  jnp.zeros((DEPTH, bp, MLSTM_HEADS, MLSTM_DH), f32),
                  jnp.zeros((DEPTH, bp, MLSTM_HEADS), f32),
                  jnp.zeros((DEPTH, bp, SSD_HEADS, SSD_HEADDIM, SSD_STATE), f32),
                  jnp.zeros((DEPTH, bp, CONV_W - 1, SSD_CONV_DIM), f32),
                  jnp.zeros((DEPTH, bp, LRU_W), f32),
                  jnp.zeros((DEPTH, bp, CONV_W - 1, LRU_W), f32))
    hp, p_state = trunk(xp, zero_state, mix_w, moe_w, norms, N_META, dtypes)
    y_prompt = hp[:, N_META:]

    xs = layer_norm(x_sample, ln_in_g, ln_in_b)
    y_sample, s_state = trunk(xs, sample_state, mix_w, moe_w, norms, 0, dtypes)

    p_mlstm_C, p_mlstm_n, p_mlstm_m, p_ssd_h, p_ssd_conv, p_lru_h, p_lru_conv = p_state
    s_mlstm_C, s_mlstm_n, s_mlstm_m, s_ssd_h, s_ssd_conv, s_lru_h, s_lru_conv = s_state
    return (y_prompt, y_sample,
            p_mlstm_C, p_mlstm_n, p_mlstm_m, p_ssd_h, p_ssd_conv, p_lru_h, p_lru_conv,
            s_mlstm_C, s_mlstm_n, s_mlstm_m, s_ssd_h, s_ssd_conv, s_lru_h, s_lru_conv)
```

```python
import functools
import math

import numpy as np
import jax
import jax.numpy as jnp
from jax import lax
from jax.experimental import pallas as pl
from jax.experimental.pallas import tpu as pltpu

F32 = jnp.float32
BF16 = jnp.bfloat16
HIGHEST = lax.Precision.HIGHEST

LANES = 128
SUBLANES = 8
VMEM_LIMIT = 56 * 1024 * 1024

CONV_W = 4
MLSTM_HEADS = 4
SSD_HEADDIM = 64
SSD_STATE = 128
SSD_GROUPS = 2
LRU_BLOCKS = 8
LRU_C = 8.0
TOP_K = 4
SWIGLU_LIMIT = 7.0
SWIGLU_ALPHA = 1.702
LN_EPS = 1e-5
RMS_EPS = 1e-6
NEG = -1e30

CHUNK = 128
SAMPLE_LS = 8
SAMPLE_SEQS = 4
ROW_TILE = 512
MOE_TM = 512
MOE_TF = 256
OUT_TILE = 256


def _cparams(sem):
    return pltpu.CompilerParams(dimension_semantics=sem, vmem_limit_bytes=VMEM_LIMIT)


def _softplus(x):
    return jnp.maximum(x, 0.0) + jnp.log1p(jnp.exp(-jnp.abs(x)))


def _log_sigmoid(x):
    return -_softplus(-x)


def _sigmoid(x):
    return 1.0 / (1.0 + jnp.exp(-x))


def _gelu_tanh(x):
    return 0.5 * x * (1.0 + jnp.tanh(math.sqrt(2.0 / math.pi) * (x + 0.044715 * (x * x * x))))


def _layer_norm(x, g, b):
    mu = jnp.mean(x, axis=-1, keepdims=True)
    xc = x - mu
    var = jnp.mean(xc * xc, axis=-1, keepdims=True)
    return xc * lax.rsqrt(var + LN_EPS) * g + b


def _dot(a, b, precision=None):
    return jnp.dot(a, b, preferred_element_type=F32, precision=precision)


def _dot_nt(a, b):
    return lax.dot_general(a, b, (((1,), (1,)), ((), ())), preferred_element_type=F32)


def _dot_tn(a, b):
    return lax.dot_general(a, b, (((0,), (0,)), ((), ())), preferred_element_type=F32)


def _ln_in_body(x_ref, g_ref, b_ref, o_ref, ob_ref):
    y = _layer_norm(x_ref[...], g_ref[...], b_ref[...])
    o_ref[...] = y
    ob_ref[...] = y.astype(BF16)


def _ln_in(x, g, b):
    T, D = x.shape
    row = pl.BlockSpec((ROW_TILE, D), lambda i: (i, 0))
    vec = pl.BlockSpec((1, D), lambda i: (0, 0))
    return pl.pallas_call(
        _ln_in_body, grid=(T // ROW_TILE,), in_specs=[row, vec, vec], out_specs=[row, row],
        out_shape=[jax.ShapeDtypeStruct((T, D), F32), jax.ShapeDtypeStruct((T, D), BF16)],
        compiler_params=_cparams(("parallel",)), name="ln_in")(x, g.reshape(1, D), b.reshape(1, D))


def _mm_body(x_ref, w_ref, o_ref):
    o_ref[...] = _dot(x_ref[...], w_ref[...]).astype(o_ref.dtype)


def _matmul(x, w, out_dtype, tn, name):
    T, K = x.shape
    N = w.shape[1]
    return pl.pallas_call(
        _mm_body, grid=(N // tn, T // ROW_TILE),
        in_specs=[pl.BlockSpec((ROW_TILE, K), lambda j, i: (i, 0)),
                  pl.BlockSpec((K, tn), lambda j, i: (0, j))],
        out_specs=pl.BlockSpec((ROW_TILE, tn), lambda j, i: (i, j)),
        out_shape=jax.ShapeDtypeStruct((T, N), out_dtype),
        compiler_params=_cparams(("parallel", "parallel")), name=name)(x, w)


def _row_ids(R, rps, chunk_rows):
    r = lax.broadcasted_iota(jnp.int32, (R, 1), 0)
    t = lax.broadcasted_iota(jnp.int32, (1, CHUNK), 1)
    if rps == R:
        j = pl.program_id(1)
        return r, t, jnp.zeros_like(r), jnp.zeros_like(t), r + j * chunk_rows, t + j * chunk_rows
    return r, t, r // rps, t // rps, r % rps, t % rps


def _conv_shift(u_bf, prev_f32, R, rps, tile_rows):
    P = prev_f32.shape[0]
    r = lax.broadcasted_iota(jnp.int32, (R, 1), 0)
    ku = lax.broadcasted_iota(jnp.int32, (1, R), 1)
    kp = lax.broadcasted_iota(jnp.int32, (1, P), 1)
    i = r % rps
    seq = r // rps
    taps = [u_bf.astype(F32)]
    for j in range(1, CONV_W):
        su = ((i >= j) & (ku == r - j)).astype(BF16)
        sp = ((i < j) & (kp == seq * tile_rows + (tile_rows - j) + i)).astype(F32)
        taps.append(_dot(su, u_bf) + _dot(sp, prev_f32, precision=HIGHEST))
    return taps


def _seq_select(seq_c, nseq, fn):
    if nseq == 1:
        return fn(0)
    out = None
    for s in range(nseq):
        term = jnp.where(seq_c == s, fn(s), 0.0)
        out = term if out is None else out + term
    return out


def _pad_rows(x, rows):
    if x.shape[0] == rows:
        return x
    return jnp.concatenate([x, jnp.zeros((rows - x.shape[0],) + x.shape[1:], x.dtype)], axis=0)


def _mlstm_body(q_ref, k_ref, v_ref, o_ref, g_ref, gt_ref, bifr_ref, bifc_ref, gm_ref,
                c0_ref, n0_ref, m0_ref, h_ref, c_ref, n_ref, m_ref, *, R, rps, nseq, l_valid):
    nh = MLSTM_HEADS
    dh = q_ref.shape[1] // nh
    j = pl.program_id(1)

    @pl.when(j == 0)
    def _():
        c_ref[...] = c0_ref[...]
        n_ref[...] = n0_ref[...]
        m_ref[...] = m0_ref[...]

    r, t, seq_c, seq_r, pos_c, pos_r = _row_ids(R, rps, R)
    valid_c = pos_c < l_valid
    valid_r = (t < R) & (pos_r < l_valid)
    same = (seq_c == seq_r) & (t < R)
    causal = same & (t <= r)

    pre_c = g_ref[...] + bifr_ref[...]
    pre_r = gt_ref[0] + bifc_ref[...]
    f_c = jnp.where(valid_c, _log_sigmoid(pre_c), 0.0)
    f_r = jnp.where(valid_r, _log_sigmoid(pre_r), 0.0)
    i_c = jnp.where(valid_c, pre_c, NEG)
    i_r = jnp.where(valid_r, pre_r, NEG)
    rq = lax.broadcasted_iota(jnp.int32, (1, R), 1)
    lower = (((r // rps) == (rq // rps)) & (rq <= r)).astype(F32) if rps != R else (rq <= r).astype(F32)
    b_c = _dot(lower, f_c, precision=HIGHEST)
    tr = lax.broadcasted_iota(jnp.int32, (CHUNK, 1), 0)
    upper = ((tr <= t) & (tr < R) & ((tr // rps) == (t // rps))).astype(F32)
    b_r = _dot(f_r, upper, precision=HIGHEST)

    lane_h = lax.broadcasted_iota(jnp.int32, (1, nh), 1)
    m_old = [m_ref[s] for s in range(nseq)]
    m_new_rows = [jnp.zeros((1, nh), F32) for _ in range(nseq)]

    for h in range(nh):
        sl = slice(h * dh, (h + 1) * dh)
        q = q_ref[:, sl]
        ks = k_ref[:, sl] * jnp.asarray(dh ** -0.5, BF16)
        v = v_ref[:, sl]
        bc = b_c[:, nh + h:nh + h + 1]
        br = b_r[nh + h:nh + h + 1, :]
        ic = i_c[:, h:h + 1]
        ir = i_r[h:h + 1, :]
        log_d = jnp.where(causal, bc - br + ir, NEG)
        mm = _seq_select(seq_c, nseq, lambda s: m_old[s][:, h:h + 1])
        log_p = bc + mm
        m_t = jnp.maximum(log_p, jnp.max(log_d, axis=1, keepdims=True))
        d = jnp.exp(log_d - m_t)
        wp = jnp.exp(log_p - m_t)
        s_mat = _dot_nt(q, _pad_rows(ks, CHUNK)) * d
        qf = q.astype(F32)
        num_state = _seq_select(seq_c, nseq, lambda s: _dot(q, c_ref[s, h].astype(BF16)))
        den_state = _seq_select(
            seq_c, nseq, lambda s: jnp.sum(qf * n_ref[s, h:h + 1, :], axis=1, keepdims=True))
        num = _dot(s_mat.astype(BF16), _pad_rows(v, CHUNK)) + wp * num_state
        den = jnp.sum(s_mat, axis=1, keepdims=True) + wp * den_state
        hh = num / jnp.maximum(jnp.abs(den), jnp.exp(-m_t))
        hn = hh * lax.rsqrt(jnp.mean(hh * hh, axis=1, keepdims=True) + RMS_EPS) * gm_ref[:, sl]
        h_ref[:, sl] = (hn * _sigmoid(o_ref[:, sl].astype(F32))).astype(h_ref.dtype)

        last = [s * rps + rps - 1 if nseq > 1 else R - 1 for s in range(nseq)]
        b_end = _seq_select(seq_c, nseq, lambda s: bc[last[s]:last[s] + 1, :])
        m_end = _seq_select(seq_c, nseq, lambda s: m_t[last[s]:last[s] + 1, :])
        w_s = jnp.exp(b_end - bc + ic - m_end)
        kw = ks.astype(F32) * w_s
        for s in range(nseq):
            be = bc[last[s]:last[s] + 1, :]
            mn = m_t[last[s]:last[s] + 1, :]
            w_prev = jnp.exp(be + m_old[s][:, h:h + 1] - mn)
            kw_s = kw if nseq == 1 else jnp.where(seq_c == s, kw, 0.0)
            c_ref[s, h] = w_prev * c_ref[s, h] + _dot_tn(kw_s.astype(BF16), v)
            n_ref[s, h:h + 1, :] = w_prev * n_ref[s, h:h + 1, :] + jnp.sum(kw_s, axis=0, keepdims=True)
            m_new_rows[s] = m_new_rows[s] + jnp.where(lane_h == h, mn, 0.0)

    for s in range(nseq):
        m_ref[s] = m_new_rows[s]


def _mlstm(pab, gates, gates_t, bif_r, bif_c, g_m, c0, n0, m0, *, B, R, rps, nseq, nc, rb0, l_valid):
    T = pab.shape[0]
    nh = MLSTM_HEADS
    W = g_m.shape[1]
    dh = W // nh

    def rowmap(col):
        return lambda b, j: (rb0 + b * nc + j, col)

    def const(b, j):
        return (0, 0)

    nb = B // nseq
    body = functools.partial(_mlstm_body, R=R, rps=rps, nseq=nseq, l_valid=l_valid)
    st4 = pl.BlockSpec((nseq, nh, dh, dh), lambda b, j: (b, 0, 0, 0))
    st3 = pl.BlockSpec((nseq, nh, dh), lambda b, j: (b, 0, 0))
    stm = pl.BlockSpec((nseq, 1, nh), lambda b, j: (b, 0, 0))
    return pl.pallas_call(
        body, grid=(nb, nc),
        in_specs=[pl.BlockSpec((R, W), rowmap(0)), pl.BlockSpec((R, W), rowmap(1)),
                  pl.BlockSpec((R, W), rowmap(2)), pl.BlockSpec((R, W), rowmap(3)),
                  pl.BlockSpec((R, LANES), rowmap(0)),
                  pl.BlockSpec((1, LANES, CHUNK), lambda b, j: (b * nc + j, 0, 0)),
                  pl.BlockSpec((1, LANES), const), pl.BlockSpec((LANES, 1), const),
                  pl.BlockSpec((1, W), const), st4, st3, stm],
        out_specs=[pl.BlockSpec((R, W), lambda b, j: (b * nc + j, 0)), st4, st3, stm],
        out_shape=[jax.ShapeDtypeStruct((nb * nc * R, W), BF16),
                   jax.ShapeDtypeStruct(c0.shape, F32), jax.ShapeDtypeStruct(n0.shape, F32),
                   jax.ShapeDtypeStruct(m0.shape, F32)],
        compiler_params=_cparams(("parallel", "arbitrary")), name="mlstm",
    )(pab, pab, pab, pab, gates, gates_t, bif_r, bif_c, g_m, c0, n0, m0)


def _ssd_body(xbc_ref, z_ref, g_ref, gt_ref, prev_ref, h0_ref, cw_ref, cb_ref, dtbr_ref, dtbc_ref,
              alr_ref, alc_ref, dsk_ref, y_ref, hs_ref, prev_s, *, R, rps, nseq, l_valid, tile_rows):
    nheads = h0_ref.shape[1]
    P = SSD_HEADDIM
    N = SSD_STATE
    G = SSD_GROUPS
    hg = nheads // G
    W = nheads * P
    j = pl.program_id(1)

    @pl.when(j == 0)
    def _():
        hs_ref[...] = h0_ref[...]
        prev_s[...] = prev_ref[...].reshape(prev_s.shape)

    r, t, seq_c, seq_r, pos_c, pos_r = _row_ids(R, rps, R)
    valid_c = pos_c < l_valid
    valid_r = (t < R) & (pos_r < l_valid)
    causal = (seq_c == seq_r) & (t < R) & (t <= r)

    u = xbc_ref[...]
    taps = _conv_shift(u, prev_s[...], R, rps, tile_rows)
    if nseq == 1:
        prev_s[...] = u[R - tile_rows:, :].astype(F32)
    acc = cb_ref[...] + taps[0] * cw_ref[CONV_W - 1:CONV_W, :]
    for jj in range(1, CONV_W):
        acc = acc + taps[jj] * cw_ref[CONV_W - 1 - jj:CONV_W - jj, :]
    xc = acc * _sigmoid(acc)
    x = xc[:, :W]
    bm = xc[:, W:W + G * N]
    cm = xc[:, W + G * N:]

    off = 2 * MLSTM_HEADS
    dt_c = jnp.where(valid_c, _softplus(g_ref[...] + dtbr_ref[...]), 0.0)
    dt_r = jnp.where(valid_r, _softplus(gt_ref[0] + dtbc_ref[...]), 0.0)
    adt_c = dt_c * (-jnp.exp(alr_ref[...]))
    adt_r = dt_r * (-jnp.exp(alc_ref[...]))
    rq = lax.broadcasted_iota(jnp.int32, (1, R), 1)
    lower = (((r // rps) == (rq // rps)) & (rq <= r)).astype(F32) if rps != R else (rq <= r).astype(F32)
    ac_c = _dot(lower, adt_c, precision=HIGHEST)
    tr = lax.broadcasted_iota(jnp.int32, (CHUNK, 1), 0)
    upper = ((tr <= t) & (tr < R) & ((tr // rps) == (t // rps))).astype(F32)
    ac_r = _dot(adt_r, upper, precision=HIGHEST)

    last = [s * rps + rps - 1 if nseq > 1 else R - 1 for s in range(nseq)]
    ac_end = _seq_select(seq_c, nseq, lambda s: ac_c[last[s]:last[s] + 1, :])
    eh = lax.broadcasted_iota(jnp.int32, (LANES, W), 0)
    el = lax.broadcasted_iota(jnp.int32, (LANES, W), 1)
    expand = (eh == off + el // P).astype(F32)
    stacked = jnp.concatenate([dt_c, jnp.exp(ac_c), jnp.exp(ac_end - ac_c)], axis=0)
    full = _dot(stacked, expand, precision=HIGHEST)
    dt_full, dec_full, tail_full = full[:R], full[R:2 * R], full[2 * R:]
    xdt = x * dt_full
    xdt_b = _pad_rows(xdt.astype(BF16), CHUNK)
    xw = (xdt * tail_full).astype(BF16)

    for g in range(G):
        cg = cm[:, g * N:(g + 1) * N].astype(BF16)
        bg = bm[:, g * N:(g + 1) * N].astype(BF16)
        cb = _dot_nt(cg, _pad_rows(bg, CHUNK))
        gsl = slice(g * hg * P, (g + 1) * hg * P)
        y_inter = _seq_select(
            seq_c, nseq,
            lambda s: _dot_nt(cg, hs_ref[s, g * hg:(g + 1) * hg].reshape(hg * P, N).astype(BF16)))
        ys = []
        for hh in range(hg):
            h = g * hg + hh
            seg = jnp.exp(jnp.where(causal, ac_c[:, off + h:off + h + 1] - ac_r[off + h:off + h + 1, :], NEG))
            ys.append(_dot((cb * seg).astype(BF16), xdt_b[:, h * P:(h + 1) * P]))
        y_g = jnp.concatenate(ys, axis=1) + y_inter * dec_full[:, gsl]
        y_g = y_g + dsk_ref[:, gsl] * x[:, gsl]
        zf = z_ref[:, gsl].astype(F32)
        y_ref[:, gsl] = (y_g * (zf * _sigmoid(zf))).astype(y_ref.dtype)
        for s in range(nseq):
            xw_s = xw[:, gsl] if nseq == 1 else jnp.where(seq_c == s, xw[:, gsl], jnp.zeros((), BF16))
            upd = _dot_tn(xw_s, bg)
            for hh in range(hg):
                h = g * hg + hh
                dec = jnp.exp(ac_c[last[s]:last[s] + 1, off + h:off + h + 1])
                hs_ref[s, h] = dec * hs_ref[s, h] + upd[hh * P:(hh + 1) * P, :]


def _ssd(pc, pab, gates, gates_t, prev, h0, conv_w, conv_b, dtb_r, dtb_c, al_r, al_c, dskip_full,
         *, B, R, rps, nseq, nc, rb0, l_valid, zcol):
    T, CW = pc.shape
    nheads = h0.shape[1]
    W = nheads * SSD_HEADDIM
    tile_rows = prev.shape[1]
    nb = B // nseq

    def rowmap(col):
        return lambda b, j: (rb0 + b * nc + j, col)

    def const(b, j):
        return (0, 0)

    body = functools.partial(_ssd_body, R=R, rps=rps, nseq=nseq, l_valid=l_valid, tile_rows=tile_rows)
    st = pl.BlockSpec((nseq, nheads, SSD_HEADDIM, SSD_STATE), lambda b, j: (b, 0, 0, 0))
    return pl.pallas_call(
        body, grid=(nb, nc),
        in_specs=[pl.BlockSpec((R, CW), rowmap(0)), pl.BlockSpec((R, W), rowmap(zcol)),
                  pl.BlockSpec((R, LANES), rowmap(0)),
                  pl.BlockSpec((1, LANES, CHUNK), lambda b, j: (b * nc + j, 0, 0)),
                  pl.BlockSpec((nseq, tile_rows, CW), lambda b, j: (b, 0, 0)), st,
                  pl.BlockSpec((CONV_W, CW), const), pl.BlockSpec((1, CW), const),
                  pl.BlockSpec((1, LANES), const), pl.BlockSpec((LANES, 1), const),
                  pl.BlockSpec((1, LANES), const), pl.BlockSpec((LANES, 1), const),
                  pl.BlockSpec((1, W), const)],
        out_specs=[pl.BlockSpec((R, W), lambda b, j: (b * nc + j, 0)), st],
        out_shape=[jax.ShapeDtypeStruct((nb * nc * R, W), BF16), jax.ShapeDtypeStruct(h0.shape, F32)],
        scratch_shapes=[pltpu.VMEM((nseq * tile_rows, CW), F32)],
        compiler_params=_cparams(("parallel", "arbitrary")), name="ssd",
    )(pc, pab, gates, gates_t, prev, h0, conv_w, conv_b, dtb_r, dtb_c, al_r, al_c, dskip_full)


def _lru_body(xl_ref, yl_ref, prev_ref, h0_ref, cw_ref, cb_ref, wa_ref, ba_ref, wi_ref, bi_ref,
              lam_ref, o_ref, hs_ref, prev_s, *, R, rps, nseq, l_valid, tile_rows):
    W = xl_ref.shape[1]
    bw = W // LRU_BLOCKS
    j = pl.program_id(1)

    @pl.when(j == 0)
    def _():
        hs_ref[...] = h0_ref[...]
        prev_s[...] = prev_ref[...].reshape(prev_s.shape)

    r, t, seq_c, seq_r, pos_c, pos_r = _row_ids(R, rps, R)
    valid_c = pos_c < l_valid
    u = xl_ref[...]
    taps = _conv_shift(u, prev_s[...], R, rps, tile_rows)
    if nseq == 1:
        prev_s[...] = u[R - tile_rows:, :].astype(F32)
    xl = cb_ref[...] + taps[0] * cw_ref[CONV_W - 1:CONV_W, :]
    for jj in range(1, CONV_W):
        xl = xl + taps[jj] * cw_ref[CONV_W - 1 - jj:CONV_W - jj, :]
    xb = xl.astype(BF16)
    ra = jnp.concatenate([_dot(xb[:, n * bw:(n + 1) * bw], wa_ref[n]) for n in range(LRU_BLOCKS)], axis=1)
    ri = jnp.concatenate([_dot(xb[:, n * bw:(n + 1) * bw], wi_ref[n]) for n in range(LRU_BLOCKS)], axis=1)
    rg = _sigmoid(ra + ba_ref[...])
    ig = _sigmoid(ri + bi_ref[...])
    log_a = -LRU_C * rg * _softplus(-lam_ref[...])
    a = jnp.where(valid_c, jnp.exp(log_a), 1.0)
    th = jnp.tanh(log_a)
    bv = jnp.where(valid_c, jnp.sqrt(-2.0 * th / (1.0 - th)) * (ig * xl), 0.0)
    i_in = r % rps if rps != R else r
    step = 1
    while step < rps:
        a_sh = pltpu.roll(a, step, 0)
        b_sh = pltpu.roll(bv, step, 0)
        m = i_in >= step
        bv = jnp.where(m, a * b_sh + bv, bv)
        a = jnp.where(m, a * a_sh, a)
        step *= 2
    h0 = _seq_select(seq_c, nseq, lambda s: hs_ref[s])
    h_seq = a * h0 + bv
    for s in range(nseq):
        lr = s * rps + rps - 1 if nseq > 1 else R - 1
        hs_ref[s] = h_seq[lr:lr + 1, :]
    o_ref[...] = (h_seq * _gelu_tanh(yl_ref[...].astype(F32))).astype(o_ref.dtype)


def _lru(pab, prev, h0, conv_w, conv_b, wa, ba, wi, bi, lam, *, B, R, rps, nseq, nc, rb0, l_valid,
         xcol, ycol):
    T = pab.shape[0]
    W = lam.shape[1]
    bw = W // LRU_BLOCKS
    tile_rows = prev.shape[1]
    nb = B // nseq

    def rowmap(col):
        return lambda b, j: (rb0 + b * nc + j, col)

    def const(b, j):
        return (0, 0)

    def const3(b, j):
        return (0, 0, 0)

    body = functools.partial(_lru_body, R=R, rps=rps, nseq=nseq, l_valid=l_valid, tile_rows=tile_rows)
    st = pl.BlockSpec((nseq, 1, W), lambda b, j: (b, 0, 0))
    vec = pl.BlockSpec((1, W), const)
    return pl.pallas_call(
        body, grid=(nb, nc),
        in_specs=[pl.BlockSpec((R, W), rowmap(xcol)), pl.BlockSpec((R, W), rowmap(ycol)),
                  pl.BlockSpec((nseq, tile_rows, W), lambda b, j: (b, 0, 0)), st,
                  pl.BlockSpec((CONV_W, W), const), vec,
                  pl.BlockSpec((LRU_BLOCKS, bw, bw), const3), vec,
                  pl.BlockSpec((LRU_BLOCKS, bw, bw), const3), vec, vec],
        out_specs=[pl.BlockSpec((R, W), lambda b, j: (b * nc + j, 0)), st],
        out_shape=[jax.ShapeDtypeStruct((nb * nc * R, W), BF16), jax.ShapeDtypeStruct(h0.shape, F32)],
        scratch_shapes=[pltpu.VMEM((nseq * tile_rows, W), F32)],
        compiler_params=_cparams(("parallel", "arbitrary")), name="lru",
    )(pab, pab, prev, h0, conv_w, conv_b, wa, ba, wi, bi, lam)


def _outproj_body(hmp_ref, hsp_ref, hlp_ref, hms_ref, hss_ref, hls_ref, x_ref, wm_ref, ws_ref, wl_ref,
                  gs_ref, g_ref, b_ref, wr_ref, br_ref, x1_ref, x1b_ref, idx_ref, gate_ref, *, alpha, npb):
    is_p = pl.program_id(0) < npb
    hm = jnp.where(is_p, hmp_ref[...], hms_ref[...])
    hl = jnp.where(is_p, hlp_ref[...], hls_ref[...])
    ysf = jnp.where(is_p, hsp_ref[...], hss_ref[...]).astype(F32)
    ysn = ysf * lax.rsqrt(jnp.mean(ysf * ysf, axis=1, keepdims=True) + RMS_EPS) * gs_ref[...]
    y = _dot(hm, wm_ref[...]) + _dot(ysn.astype(BF16), ws_ref[...]) + _dot(hl, wl_ref[...])
    x1 = _layer_norm(alpha * x_ref[...] + y, g_ref[...], b_ref[...])
    x1b = x1.astype(BF16)
    x1_ref[...] = x1
    x1b_ref[...] = x1b
    logits = _dot(x1b, wr_ref[...]) + br_ref[...]
    lane = lax.broadcasted_iota(jnp.int32, logits.shape, 1)
    idx_out = jnp.zeros(logits.shape, jnp.int32)
    val_out = jnp.zeros(logits.shape, F32)
    top = None
    for k in range(TOP_K):
        mx = jnp.max(logits, axis=1, keepdims=True)
        sel = jnp.min(jnp.where(logits == mx, lane, LANES), axis=1, keepdims=True)
        if top is None:
            top = mx
        idx_out = jnp.where(lane == k, sel, idx_out)
        val_out = jnp.where(lane == k, jnp.exp(mx - top), val_out)
        logits = jnp.where(lane == sel, -jnp.inf, logits)
    gate_ref[...] = val_out / jnp.sum(val_out, axis=1, keepdims=True)
    idx_ref[...] = idx_out


def _outproj(mix_p, mix_s, x, wm, ws, wl, g_ssd, ln_g, ln_b, wr, br, alpha):
    T, D = x.shape
    tm = OUT_TILE
    npb = mix_p[0].shape[0] // tm
    assert npb * tm == mix_p[0].shape[0] and mix_s[0].shape[0] % tm == 0

    def rows(w):
        return pl.BlockSpec((tm, w), lambda i: (i, 0))

    def rows_p(a):
        return pl.BlockSpec((tm, a.shape[1]), lambda i: (jnp.minimum(i, npb - 1), 0))

    def rows_s(a):
        return pl.BlockSpec((tm, a.shape[1]), lambda i: (jnp.maximum(i - npb, 0), 0))

    def whole(a):
        return pl.BlockSpec(a.shape, lambda i: (0, 0), pipeline_mode=pl.Buffered(1))

    return pl.pallas_call(
        functools.partial(_outproj_body, alpha=alpha, npb=npb), grid=(T // tm,),
        in_specs=[rows_p(mix_p[0]), rows_p(mix_p[1]), rows_p(mix_p[2]),
                  rows_s(mix_s[0]), rows_s(mix_s[1]), rows_s(mix_s[2]), rows(D),
                  whole(wm), whole(ws), whole(wl), whole(g_ssd), whole(ln_g), whole(ln_b),
                  whole(wr), whole(br)],
        out_specs=[rows(D), rows(D), rows(LANES), rows(LANES)],
        out_shape=[jax.ShapeDtypeStruct((T, D), F32), jax.ShapeDtypeStruct((T, D), BF16),
                   jax.ShapeDtypeStruct((T, LANES), jnp.int32), jax.ShapeDtypeStruct((T, LANES), F32)],
        compiler_params=_cparams(("parallel",)), name="outproj_ln_router",
    )(*mix_p, *mix_s, x, wm, ws, wl, g_ssd, ln_g, ln_b, wr, br)


def _expert_body(be_ref, na_ref, x_ref, wg_ref, bg_ref, wu_ref, bu_ref, wd_ref, bd_ref, o_ref):
    i = pl.program_id(0)
    j = pl.program_id(1)

    @pl.when(i < na_ref[0])
    def _():
        xb = x_ref[...]
        gate = jnp.minimum(_dot(xb, wg_ref[0].astype(BF16)) + bg_ref[0], SWIGLU_LIMIT)
        up = jnp.clip(_dot(xb, wu_ref[0].astype(BF16)) + bu_ref[0], -SWIGLU_LIMIT, SWIGLU_LIMIT)
        act = (up + 1.0) * gate * _sigmoid(SWIGLU_ALPHA * gate)
        contrib = _dot(act.astype(BF16), wd_ref[0].astype(BF16))

        @pl.when(j == 0)
        def _():
            o_ref[...] = contrib + bd_ref[0]

        @pl.when(j != 0)
        def _():
            o_ref[...] += contrib

    @pl.when((i >= na_ref[0]) & (j == 0))
    def _():
        o_ref[...] = jnp.zeros_like(o_ref)


def _experts(xs, blk_e, n_active, wg, bg, wu, bu, wd, bd):
    n_rows, D = xs.shape
    E, _, Fh = wg.shape
    nblk = n_rows // MOE_TM
    nf = Fh // MOE_TF

    def jeff(i, j, na):
        return jnp.where(i < na[0], j, nf - 1)

    grid_spec = pltpu.PrefetchScalarGridSpec(
        num_scalar_prefetch=2, grid=(nblk, nf),
        in_specs=[pl.BlockSpec((MOE_TM, D), lambda i, j, be, na: (i, 0)),
                  pl.BlockSpec((1, D, MOE_TF), lambda i, j, be, na: (be[i], 0, jeff(i, j, na))),
                  pl.BlockSpec((1, 1, MOE_TF), lambda i, j, be, na: (be[i], 0, jeff(i, j, na))),
                  pl.BlockSpec((1, D, MOE_TF), lambda i, j, be, na: (be[i], 0, jeff(i, j, na))),
                  pl.BlockSpec((1, 1, MOE_TF), lambda i, j, be, na: (be[i], 0, jeff(i, j, na))),
                  pl.BlockSpec((1, MOE_TF, D), lambda i, j, be, na: (be[i], jeff(i, j, na), 0)),
                  pl.BlockSpec((1, 1, D), lambda i, j, be, na: (be[i], 0, 0))],
        out_specs=pl.BlockSpec((MOE_TM, D), lambda i, j, be, na: (i, 0)))
    return pl.pallas_call(
        _expert_body, grid_spec=grid_spec, out_shape=jax.ShapeDtypeStruct((n_rows, D), F32),
        compiler_params=_cparams(("parallel", "arbitrary")), name="moe_experts",
    )(blk_e, n_active, xs, wg, bg.reshape(E, 1, Fh), wu, bu.reshape(E, 1, Fh), wd, bd.reshape(E, 1, D))


def _combine_body(yg_ref, gate_ref, x_ref, g_ref, b_ref, o_ref, ob_ref, *, alpha):
    gates = gate_ref[...]
    f = gates[:, 0:1] * yg_ref[0]
    for k in range(1, TOP_K):
        f = f + gates[:, k:k + 1] * yg_ref[k]
    x2 = _layer_norm(alpha * x_ref[...] + f, g_ref[...], b_ref[...])
    o_ref[...] = x2
    ob_ref[...] = x2.astype(BF16)


def _combine_ln(yg, gates, x, ln_g, ln_b, alpha):
    T, D = x.shape
    tm = OUT_TILE
    rows = pl.BlockSpec((tm, D), lambda i: (i, 0))
    vec = pl.BlockSpec((1, D), lambda i: (0, 0))
    return pl.pallas_call(
        functools.partial(_combine_body, alpha=alpha), grid=(T // tm,),
        in_specs=[pl.BlockSpec((TOP_K, tm, D), lambda i: (0, i, 0)),
                  pl.BlockSpec((tm, LANES), lambda i: (i, 0)), rows, vec, vec],
        out_specs=[rows, rows],
        out_shape=[jax.ShapeDtypeStruct((T, D), F32), jax.ShapeDtypeStruct((T, D), BF16)],
        compiler_params=_cparams(("parallel",)), name="moe_combine_ln",
    )(yg, gates, x, ln_g, ln_b)


def _moe_dispatch(idx, valid_tok, n_experts, n_rows):
    T = idx.shape[0]
    flat_e = jnp.where(valid_tok[:, None], idx[:, :TOP_K], n_experts).reshape(-1)
    order = jnp.argsort(flat_e)
    sorted_e = flat_e[order]
    counts = jnp.bincount(flat_e, length=n_experts + 1)[:n_experts]
    padded = (counts + MOE_TM - 1) // MOE_TM * MOE_TM
    pad_end = jnp.cumsum(padded)
    pad_start = pad_end - padded
    start = jnp.cumsum(counts) - counts
    e_clip = jnp.minimum(sorted_e, n_experts - 1)
    dest_sorted = pad_start[e_clip] + jnp.arange(T * TOP_K) - start[e_clip]
    dest_sorted = jnp.where(sorted_e < n_experts, dest_sorted, n_rows).astype(jnp.int32)
    row_tok = jnp.zeros((n_rows,), jnp.int32).at[dest_sorted].set((order // TOP_K).astype(jnp.int32), mode="drop")
    nblk = n_rows // MOE_TM
    blk_e = jnp.searchsorted(pad_end, jnp.arange(nblk) * MOE_TM, side="right")
    n_active = (pad_end[-1] // MOE_TM).astype(jnp.int32)
    last_e = jnp.minimum(blk_e[jnp.maximum(n_active - 1, 0)], n_experts - 1)
    blk_e = jnp.where(jnp.arange(nblk) < n_active, jnp.minimum(blk_e, n_experts - 1), last_e).astype(jnp.int32)
    dest = jnp.zeros((T * TOP_K,), jnp.int32).at[order].set(jnp.minimum(dest_sorted, n_rows - 1))
    return row_tok, blk_e, n_active.reshape(1), dest.reshape(T, TOP_K)


def kernel(x_prompt, x_sample, state_mlstm_C, state_mlstm_n, state_mlstm_m, state_ssd_h, state_ssd_conv,
           state_lru_h, state_lru_conv, meta, ln_in_g, ln_in_b, w_in, b_if, conv_ssd_w, conv_ssd_b, dt_bias,
           a_log, d_skip, g_ssd, g_mlstm, conv_lru_w, conv_lru_b, w_rg_a, b_rg_a, w_rg_i, b_rg_i, lam, w_out,
           ln1_g, ln1_b, w_router, b_router, w_gate, b_gate, w_up, b_up, w_down, b_down, ln2_g, ln2_b):
    Bp, S, D = x_prompt.shape
    Bs, Ls, _ = x_sample.shape
    depth = w_in.shape[0]
    n_meta = meta.shape[0]
    nh = MLSTM_HEADS
    MW = g_mlstm.shape[1]
    SW = g_ssd.shape[1]
    LW = lam.shape[1]
    CW = conv_ssd_w.shape[2]
    n_ssd_heads = dt_bias.shape[1]
    E = w_router.shape[2]
    alpha = (2 * depth) ** 0.25
    Lp = n_meta + S
    LP = -(-Lp // CHUNK) * CHUNK
    LS = SAMPLE_LS
    Tp, Ts = Bp * LP, Bs * LS
    T = Tp + Ts
    ncp = LP // CHUNK
    RS = SAMPLE_SEQS * LS
    assert CONV_W - 1 <= Ls <= LS and T % ROW_TILE == 0 and T % OUT_TILE == 0 and Bs % SAMPLE_SEQS == 0 and Tp % RS == 0

    xp = jnp.concatenate([jnp.broadcast_to(meta.astype(F32)[None], (Bp, n_meta, D)), x_prompt,
                          jnp.zeros((Bp, LP - Lp, D), F32)], axis=1).reshape(Tp, D)
    xs = jnp.concatenate([x_sample, jnp.zeros((Bs, LS - Ls, D), F32)], axis=1).reshape(Ts, D)
    x, xb = _ln_in(jnp.concatenate([xp, xs], axis=0), ln_in_g, ln_in_b)
    pos = np.concatenate([np.tile(np.arange(LP) < Lp, Bp), np.tile(np.arange(LS) < Ls, Bs)])
    valid_tok = jnp.asarray(pos)

    sizes = [MW, MW, MW, MW, nh, nh, SW, CW, n_ssd_heads, LW, LW]
    o = [0] + [int(v) for v in np.cumsum(sizes)]
    w_ab = jnp.concatenate([w_in[:, :, o[0]:o[4]], w_in[:, :, o[6]:o[7]], w_in[:, :, o[9]:o[11]]],
                           axis=2).astype(BF16)
    w_c = w_in[:, :, o[7]:o[8]].astype(BF16)
    n_gate = 2 * nh + n_ssd_heads
    w_d = jnp.concatenate([w_in[:, :, o[4]:o[6]], w_in[:, :, o[8]:o[9]],
                           jnp.zeros((depth, D, LANES - n_gate), F32)], axis=2).astype(BF16)
    zcol = (4 * MW) // SW
    xcol = (4 * MW + SW) // LW
    ycol = xcol + 1
    assert zcol * SW == 4 * MW and xcol * LW == 4 * MW + SW

    def lane_pad(v, fill=0.0, front=0):
        return jnp.concatenate([jnp.full((front,), fill, F32), v.astype(F32),
                                jnp.full((LANES - front - v.shape[0],), fill, F32)])

    w_out_b = w_out.astype(BF16)
    wr_b = jnp.concatenate([w_router, jnp.zeros((depth, D, LANES - E), F32)], axis=2).astype(BF16)
    n_assign = (Bp * Lp + Bs * Ls) * TOP_K
    n_rows = -(-(n_assign + E * (MOE_TM - 1)) // MOE_TM) * MOE_TM

    zeros_p = dict(
        c=jnp.zeros((Bp, nh, MW // nh, MW // nh), F32), n=jnp.zeros((Bp, nh, MW // nh), F32),
        m=jnp.zeros((Bp, 1, nh), F32), h=jnp.zeros((Bp, n_ssd_heads, SSD_HEADDIM, SSD_STATE), F32),
        cs=jnp.zeros((Bp, 16, CW), F32), l=jnp.zeros((Bp, 1, LW), F32), cl=jnp.zeros((Bp, 16, LW), F32))
    pgrp = dict(B=Bp, R=CHUNK, rps=CHUNK, nseq=1, nc=ncp, rb0=0, l_valid=Lp)
    sgrp = dict(B=Bs, R=RS, rps=LS, nseq=SAMPLE_SEQS, nc=1, rb0=Tp // RS, l_valid=Ls)

    outs = [[] for _ in range(14)]
    for l in range(depth):
        pab = _matmul(xb, w_ab[l], BF16, 1024, "inproj_ab")
        pc = _matmul(xb, w_c[l], BF16, CW // 2, "inproj_c")
        gates = _matmul(xb, w_d[l], F32, LANES, "inproj_gates")
        gt_p = gates[:Tp].reshape(Bp * ncp, CHUNK, LANES).transpose(0, 2, 1)
        gt_s = gates[Tp:].reshape(Bs // SAMPLE_SEQS, RS, LANES).transpose(0, 2, 1)
        gt_s = jnp.concatenate([gt_s, jnp.zeros((Bs // SAMPLE_SEQS, LANES, CHUNK - RS), F32)], axis=2)

        bif_r = lane_pad(b_if[l]).reshape(1, LANES)
        bif_c = bif_r.reshape(LANES, 1)
        dtb_r = lane_pad(dt_bias[l], front=2 * nh).reshape(1, LANES)
        al_r = lane_pad(a_log[l], front=2 * nh).reshape(1, LANES)
        dsk = jnp.repeat(d_skip[l].astype(F32), SSD_HEADDIM).reshape(1, SW)
        gm = g_mlstm[l].reshape(1, MW).astype(F32)
        cw_s, cb_s = conv_ssd_w[l].astype(F32), conv_ssd_b[l].reshape(1, CW).astype(F32)
        cw_l, cb_l = conv_lru_w[l].astype(F32), conv_lru_b[l].reshape(1, LW).astype(F32)
        wa, wi = w_rg_a[l].astype(BF16), w_rg_i[l].astype(BF16)
        ba, bi = b_rg_a[l].reshape(1, LW).astype(F32), b_rg_i[l].reshape(1, LW).astype(F32)
        lam_l = lam[l].reshape(1, LW).astype(F32)

        def pad_conv(buf):
            return jnp.concatenate([jnp.zeros((buf.shape[0], SUBLANES - (CONV_W - 1), buf.shape[2]), F32),
                                    buf.astype(F32)], axis=1)

        st_s = dict(c=state_mlstm_C[l].astype(F32), n=state_mlstm_n[l].astype(F32),
                    m=state_mlstm_m[l].astype(F32).reshape(Bs, 1, nh), h=state_ssd_h[l].astype(F32),
                    cs=pad_conv(state_ssd_conv[l]), l=state_lru_h[l].astype(F32).reshape(Bs, 1, LW),
                    cl=pad_conv(state_lru_conv[l]))

        mixes, new_states = [], []
        for grp, st, gt in ((pgrp, zeros_p, gt_p), (sgrp, st_s, gt_s)):
            hm_g, c_n, n_n, m_n = _mlstm(pab, gates, gt, bif_r, bif_c, gm, st["c"], st["n"], st["m"], **grp)
            hs_g, h_n = _ssd(pc, pab, gates, gt, st["cs"], st["h"], cw_s, cb_s, dtb_r, dtb_r.reshape(LANES, 1),
                             al_r, al_r.reshape(LANES, 1), dsk, zcol=zcol, **grp)
            hl_g, l_n = _lru(pab, st["cl"], st["l"], cw_l, cb_l, wa, ba, wi, bi, lam_l, xcol=xcol, ycol=ycol,
                             **grp)
            new_states.append((c_n, n_n, m_n, h_n, l_n))
            mixes.append((hm_g, hs_g, hl_g))

        x1, x1b, idx, gate_w = _outproj(
            mixes[0], mixes[1], x, w_out_b[l, :MW], w_out_b[l, MW:MW + SW], w_out_b[l, MW + SW:],
            g_ssd[l].reshape(1, SW).astype(F32), ln1_g[l].reshape(1, D), ln1_b[l].reshape(1, D),
            wr_b[l], lane_pad(b_router[l], fill=NEG).reshape(1, LANES), alpha)

        row_tok, blk_e, n_active, dest = _moe_dispatch(idx, valid_tok, E, n_rows)
        yb = _experts(x1b[row_tok], blk_e, n_active, w_gate[l], b_gate[l], w_up[l], b_up[l], w_down[l], b_down[l])
        yg = yb[dest.T]
        x, xb = _combine_ln(yg, gate_w, x1, ln2_g[l].reshape(1, D), ln2_b[l].reshape(1, D), alpha)

        xbc_p = pc[:Tp].reshape(Bp, LP, CW)[:, Lp - (CONV_W - 1):Lp].astype(F32)
        xbc_s = pc[Tp:].reshape(Bs, LS, CW)[:, Ls - (CONV_W - 1):Ls].astype(F32)
        xl_p = pab[:Tp, xcol * LW:(xcol + 1) * LW].reshape(Bp, LP, LW)[:, Lp - (CONV_W - 1):Lp].astype(F32)
        xl_s = pab[Tp:, xcol * LW:(xcol + 1) * LW].reshape(Bs, LS, LW)[:, Ls - (CONV_W - 1):Ls].astype(F32)
        for gi, (bsz, conv_s, conv_l) in enumerate(((Bp, xbc_p, xl_p), (Bs, xbc_s, xl_s))):
            c_n, n_n, m_n, h_n, l_n = new_states[gi]
            vals = (c_n, n_n, m_n.reshape(bsz, nh), h_n, conv_s, l_n.reshape(bsz, LW), conv_l)
            for k, v in enumerate(vals):
                outs[gi * 7 + k].append(v)

    y_prompt = x[:Tp].reshape(Bp, LP, D)[:, n_meta:Lp]
    y_sample = x[Tp:].reshape(Bs, LS, D)[:, :Ls]
    state_dtypes = [state_mlstm_C.dtype, state_mlstm_n.dtype, state_mlstm_m.dtype, state_ssd_h.dtype,
                    state_ssd_conv.dtype, state_lru_h.dtype, state_lru_conv.dtype]
    stacked = [jnp.stack(v).astype(state_dtypes[k % 7]) for k, v in enumerate(outs)]
    return (y_prompt, y_sample, *stacked)
```

```python
import functools
import math

import numpy as np
import jax
import jax.numpy as jnp
from jax import lax
from jax.experimental import pallas as pl
from jax.experimental.pallas import tpu as pltpu

F32 = jnp.float32
BF16 = jnp.bfloat16
HIGHEST = lax.Precision.HIGHEST

LANES = 128
SUBLANES = 8
VMEM_LIMIT = 56 * 1024 * 1024

CONV_W = 4
MLSTM_HEADS = 4
SSD_HEADDIM = 64
SSD_STATE = 128
SSD_GROUPS = 2
LRU_BLOCKS = 8
LRU_C = 8.0
TOP_K = 4
SWIGLU_LIMIT = 7.0
SWIGLU_ALPHA = 1.702
LN_EPS = 1e-5
RMS_EPS = 1e-6
NEG = -1e30

CHUNK = 128
SAMPLE_LS = 8
SAMPLE_SEQS = 4
ROW_TILE = 512
MOE_TM = 512
MOE_TF = 256
OUT_TILE = 256


def _cparams(sem):
    return pltpu.CompilerParams(dimension_semantics=sem, vmem_limit_bytes=VMEM_LIMIT)


def _softplus(x):
    return jnp.maximum(x, 0.0) + jnp.log1p(jnp.exp(-jnp.abs(x)))


def _log_sigmoid(x):
    return -_softplus(-x)


def _sigmoid(x):
    return 1.0 / (1.0 + jnp.exp(-x))


def _gelu_tanh(x):
    return 0.5 * x * (1.0 + jnp.tanh(math.sqrt(2.0 / math.pi) * (x + 0.044715 * (x * x * x))))


def _layer_norm(x, g, b):
    mu = jnp.mean(x, axis=-1, keepdims=True)
    xc = x - mu
    var = jnp.mean(xc * xc, axis=-1, keepdims=True)
    return xc * lax.rsqrt(var + LN_EPS) * g + b


def _dot(a, b, precision=None):
    return jnp.dot(a, b, preferred_element_type=F32, precision=precision)


def _dot_nt(a, b):
    return lax.dot_general(a, b, (((1,), (1,)), ((), ())), preferred_element_type=F32)


def _dot_tn(a, b):
    return lax.dot_general(a, b, (((0,), (0,)), ((), ())), preferred_element_type=F32)


def _ln_in_body(x_ref, g_ref, b_ref, o_ref, ob_ref):
    y = _layer_norm(x_ref[...], g_ref[...], b_ref[...])
    o_ref[...] = y
    ob_ref[...] = y.astype(BF16)


def _ln_in(x, g, b):
    T, D = x.shape
    row = pl.BlockSpec((ROW_TILE, D), lambda i: (i, 0))
    vec = pl.BlockSpec((1, D), lambda i: (0, 0))
    return pl.pallas_call(
        _ln_in_body, grid=(T // ROW_TILE,), in_specs=[row, vec, vec], out_specs=[row, row],
        out_shape=[jax.ShapeDtypeStruct((T, D), F32), jax.ShapeDtypeStruct((T, D), BF16)],
        compiler_params=_cparams(("parallel",)), name="ln_in")(x, g.reshape(1, D), b.reshape(1, D))


def _mm_body(x_ref, w_ref, o_ref):
    o_ref[...] = _dot(x_ref[...], w_ref[0]).astype(o_ref.dtype)


def _matmul(x, w, l, out_dtype, tn, name):
    T, K = x.shape
    N = w.shape[2]
    return pl.pallas_call(
        _mm_body, grid=(N // tn, T // ROW_TILE),
        in_specs=[pl.BlockSpec((ROW_TILE, K), lambda j, i: (i, 0)),
                  pl.BlockSpec((1, K, tn), lambda j, i: (l, 0, j))],
        out_specs=pl.BlockSpec((ROW_TILE, tn), lambda j, i: (i, j)),
        out_shape=jax.ShapeDtypeStruct((T, N), out_dtype),
        compiler_params=_cparams(("parallel", "parallel")), name=name)(x, w)


def _row_ids(R, rps, chunk_rows):
    r = lax.broadcasted_iota(jnp.int32, (R, 1), 0)
    t = lax.broadcasted_iota(jnp.int32, (1, CHUNK), 1)
    if rps == R:
        j = pl.program_id(1)
        return r, t, jnp.zeros_like(r), jnp.zeros_like(t), r + j * chunk_rows, t + j * chunk_rows
    return r, t, r // rps, t // rps, r % rps, t % rps


def _conv_shift(u_bf, prev_f32, R, rps, tile_rows):
    P = prev_f32.shape[0]
    r = lax.broadcasted_iota(jnp.int32, (R, 1), 0)
    ku = lax.broadcasted_iota(jnp.int32, (1, R), 1)
    kp = lax.broadcasted_iota(jnp.int32, (1, P), 1)
    i = r % rps
    seq = r // rps
    taps = [u_bf.astype(F32)]
    for j in range(1, CONV_W):
        su = ((i >= j) & (ku == r - j)).astype(BF16)
        sp = ((i < j) & (kp == seq * tile_rows + (tile_rows - j) + i)).astype(F32)
        taps.append(_dot(su, u_bf) + _dot(sp, prev_f32, precision=HIGHEST))
    return taps


def _seq_select(seq_c, nseq, fn):
    if nseq == 1:
        return fn(0)
    out = None
    for s in range(nseq):
        term = jnp.where(seq_c == s, fn(s), 0.0)
        out = term if out is None else out + term
    return out


def _pad_rows(x, rows):
    if x.shape[0] == rows:
        return x
    return jnp.concatenate([x, jnp.zeros((rows - x.shape[0],) + x.shape[1:], x.dtype)], axis=0)


def _mlstm_body(q_ref, k_ref, v_ref, o_ref, g_ref, gt_ref, bifr_ref, bifc_ref, gm_ref,
                n0_ref, m0_ref, c0_ref, h_ref, c_ref, n_ref, m_ref, *, R, rps, nseq, l_valid):
    nh = MLSTM_HEADS
    dh = q_ref.shape[1] // nh
    j = pl.program_id(1)

    @pl.when(j == 0)
    def _():
        c_ref[...] = c0_ref[...]
        n_ref[...] = n0_ref[...]
        m_ref[...] = m0_ref[...]

    r, t, seq_c, seq_r, pos_c, pos_r = _row_ids(R, rps, R)
    valid_c = pos_c < l_valid
    valid_r = (t < R) & (pos_r < l_valid)
    same = (seq_c == seq_r) & (t < R)
    causal = same & (t <= r)

    pre_c = g_ref[...] + bifr_ref[...]
    pre_r = gt_ref[0] + bifc_ref[...]
    f_c = jnp.where(valid_c, _log_sigmoid(pre_c), 0.0)
    f_r = jnp.where(valid_r, _log_sigmoid(pre_r), 0.0)
    i_c = jnp.where(valid_c, pre_c, NEG)
    i_r = jnp.where(valid_r, pre_r, NEG)
    rq = lax.broadcasted_iota(jnp.int32, (1, R), 1)
    lower = (((r // rps) == (rq // rps)) & (rq <= r)).astype(F32) if rps != R else (rq <= r).astype(F32)
    b_c = _dot(lower, f_c, precision=HIGHEST)
    tr = lax.broadcasted_iota(jnp.int32, (CHUNK, 1), 0)
    upper = ((tr <= t) & (tr < R) & ((tr // rps) == (t // rps))).astype(F32)
    b_r = _dot(f_r, upper, precision=HIGHEST)

    lane_h = lax.broadcasted_iota(jnp.int32, (1, nh), 1)
    m_old = [m_ref[s] for s in range(nseq)]
    m_new_rows = [jnp.zeros((1, nh), F32) for _ in range(nseq)]

    for h in range(nh):
        sl = slice(h * dh, (h + 1) * dh)
        q = q_ref[:, sl]
        ks = k_ref[:, sl] * jnp.asarray(dh ** -0.5, BF16)
        v = v_ref[:, sl]
        bc = b_c[:, nh + h:nh + h + 1]
        br = b_r[nh + h:nh + h + 1, :]
        ic = i_c[:, h:h + 1]
        ir = i_r[h:h + 1, :]
        log_d = jnp.where(causal, bc - br + ir, NEG)
        mm = _seq_select(seq_c, nseq, lambda s: m_old[s][:, h:h + 1])
        log_p = bc + mm
        m_t = jnp.maximum(log_p, jnp.max(log_d, axis=1, keepdims=True))
        d = jnp.exp(log_d - m_t)
        wp = jnp.exp(log_p - m_t)
        s_mat = _dot_nt(q, _pad_rows(ks, CHUNK)) * d
        qf = q.astype(F32)
        num_state = _seq_select(seq_c, nseq, lambda s: _dot(q, c_ref[0, s, h].astype(BF16)))
        den_state = _seq_select(
            seq_c, nseq, lambda s: jnp.sum(qf * n_ref[s, h:h + 1, :], axis=1, keepdims=True))
        num = _dot(s_mat.astype(BF16), _pad_rows(v, CHUNK)) + wp * num_state
        den = jnp.sum(s_mat, axis=1, keepdims=True) + wp * den_state
        hh = num / jnp.maximum(jnp.abs(den), jnp.exp(-m_t))
        hn = hh * lax.rsqrt(jnp.mean(hh * hh, axis=1, keepdims=True) + RMS_EPS) * gm_ref[:, sl]
        h_ref[:, sl] = (hn * _sigmoid(o_ref[:, sl].astype(F32))).astype(h_ref.dtype)

        last = [s * rps + rps - 1 if nseq > 1 else R - 1 for s in range(nseq)]
        b_end = _seq_select(seq_c, nseq, lambda s: bc[last[s]:last[s] + 1, :])
        m_end = _seq_select(seq_c, nseq, lambda s: m_t[last[s]:last[s] + 1, :])
        w_s = jnp.exp(b_end - bc + ic - m_end)
        kw = ks.astype(F32) * w_s
        for s in range(nseq):
            be = bc[last[s]:last[s] + 1, :]
            mn = m_t[last[s]:last[s] + 1, :]
            w_prev = jnp.exp(be + m_old[s][:, h:h + 1] - mn)
            kw_s = kw if nseq == 1 else jnp.where(seq_c == s, kw, 0.0)
            c_ref[0, s, h] = w_prev * c_ref[0, s, h] + _dot_tn(kw_s.astype(BF16), v)
            n_ref[s, h:h + 1, :] = w_prev * n_ref[s, h:h + 1, :] + jnp.sum(kw_s, axis=0, keepdims=True)
            m_new_rows[s] = m_new_rows[s] + jnp.where(lane_h == h, mn, 0.0)

    for s in range(nseq):
        m_ref[s] = m_new_rows[s]


def _drop_alias(body, n_in):
    def wrapped(*refs):
        return body(*refs[:n_in], *refs[n_in + 1:])
    return wrapped


def _layered_state_call(body, grid, in_specs, args, out_specs, out_shapes, big_out, state_in, state_blk,
                        layers, name, scratch_shapes=()):
    l_in, l_out, n_out_layers, prev_out = layers
    zeros = (0,) * (len(state_blk) - 1)
    in_specs = list(in_specs) + [pl.BlockSpec((1,) + state_blk, lambda b, j: (l_in, b) + zeros)]
    args = list(args) + [state_in]
    out_specs = list(out_specs)
    out_specs.insert(big_out, pl.BlockSpec((1,) + state_blk, lambda b, j: (l_out, b) + zeros))
    out_shapes = list(out_shapes)
    out_shapes.insert(big_out, jax.ShapeDtypeStruct((n_out_layers,) + state_in.shape[1:], F32))
    aliases = {}
    if prev_out is not None:
        body = _drop_alias(body, len(args))
        aliases = {len(args): big_out}
        in_specs.append(pl.BlockSpec(memory_space=pl.ANY))
        args.append(prev_out)
    return pl.pallas_call(
        body, grid=grid, in_specs=in_specs, out_specs=out_specs, out_shape=out_shapes,
        input_output_aliases=aliases, scratch_shapes=list(scratch_shapes),
        compiler_params=_cparams(("parallel", "arbitrary")), name=name)(*args)


def _mlstm(pab, gates, gates_t, bif_r, bif_c, g_m, n0, m0, c_all, layers, *, B, R, rps, nseq, nc, rb0,
           l_valid):
    nh = MLSTM_HEADS
    W = g_m.shape[1]
    dh = W // nh

    def rowmap(col):
        return lambda b, j: (rb0 + b * nc + j, col)

    def const(b, j):
        return (0, 0)

    nb = B // nseq
    body = functools.partial(_mlstm_body, R=R, rps=rps, nseq=nseq, l_valid=l_valid)
    st3 = pl.BlockSpec((nseq, nh, dh), lambda b, j: (b, 0, 0))
    stm = pl.BlockSpec((nseq, 1, nh), lambda b, j: (b, 0, 0))
    return _layered_state_call(
        body, (nb, nc),
        [pl.BlockSpec((R, W), rowmap(0)), pl.BlockSpec((R, W), rowmap(1)),
         pl.BlockSpec((R, W), rowmap(2)), pl.BlockSpec((R, W), rowmap(3)),
         pl.BlockSpec((R, LANES), rowmap(0)),
         pl.BlockSpec((1, LANES, CHUNK), lambda b, j: (b * nc + j, 0, 0)),
         pl.BlockSpec((1, LANES), const), pl.BlockSpec((LANES, 1), const),
         pl.BlockSpec((1, W), const), st3, stm],
        [pab, pab, pab, pab, gates, gates_t, bif_r, bif_c, g_m, n0, m0],
        [pl.BlockSpec((R, W), lambda b, j: (b * nc + j, 0)), st3, stm],
        [jax.ShapeDtypeStruct((nb * nc * R, W), BF16), jax.ShapeDtypeStruct(n0.shape, F32),
         jax.ShapeDtypeStruct(m0.shape, F32)],
        1, c_all, (nseq, nh, dh, dh), layers, "mlstm")


def _ssd_body(xbc_ref, z_ref, g_ref, gt_ref, prev_ref, cw_ref, cb_ref, dtbr_ref, dtbc_ref,
              alr_ref, alc_ref, dsk_ref, h0_ref, y_ref, hs_ref, prev_s, *, R, rps, nseq, l_valid, tile_rows):
    nheads = h0_ref.shape[2]
    P = SSD_HEADDIM
    N = SSD_STATE
    G = SSD_GROUPS
    hg = nheads // G
    W = nheads * P
    j = pl.program_id(1)

    @pl.when(j == 0)
    def _():
        hs_ref[...] = h0_ref[...]
        prev_s[...] = prev_ref[...].reshape(prev_s.shape)

    r, t, seq_c, seq_r, pos_c, pos_r = _row_ids(R, rps, R)
    valid_c = pos_c < l_valid
    valid_r = (t < R) & (pos_r < l_valid)
    causal = (seq_c == seq_r) & (t < R) & (t <= r)

    u = xbc_ref[...]
    taps = _conv_shift(u, prev_s[...], R, rps, tile_rows)
    if nseq == 1:
        prev_s[...] = u[R - tile_rows:, :].astype(F32)
    acc = cb_ref[...] + taps[0] * cw_ref[CONV_W - 1:CONV_W, :]
    for jj in range(1, CONV_W):
        acc = acc + taps[jj] * cw_ref[CONV_W - 1 - jj:CONV_W - jj, :]
    xc = acc * _sigmoid(acc)
    x = xc[:, :W]
    bm = xc[:, W:W + G * N]
    cm = xc[:, W + G * N:]

    off = 2 * MLSTM_HEADS
    dt_c = jnp.where(valid_c, _softplus(g_ref[...] + dtbr_ref[...]), 0.0)
    dt_r = jnp.where(valid_r, _softplus(gt_ref[0] + dtbc_ref[...]), 0.0)
    adt_c = dt_c * (-jnp.exp(alr_ref[...]))
    adt_r = dt_r * (-jnp.exp(alc_ref[...]))
    rq = lax.broadcasted_iota(jnp.int32, (1, R), 1)
    lower = (((r // rps) == (rq // rps)) & (rq <= r)).astype(F32) if rps != R else (rq <= r).astype(F32)
    ac_c = _dot(lower, adt_c, precision=HIGHEST)
    tr = lax.broadcasted_iota(jnp.int32, (CHUNK, 1), 0)
    upper = ((tr <= t) & (tr < R) & ((tr // rps) == (t // rps))).astype(F32)
    ac_r = _dot(adt_r, upper, precision=HIGHEST)

    last = [s * rps + rps - 1 if nseq > 1 else R - 1 for s in range(nseq)]
    ac_end = _seq_select(seq_c, nseq, lambda s: ac_c[last[s]:last[s] + 1, :])
    eh = lax.broadcasted_iota(jnp.int32, (LANES, W), 0)
    el = lax.broadcasted_iota(jnp.int32, (LANES, W), 1)
    expand = (eh == off + el // P).astype(F32)
    stacked = jnp.concatenate([dt_c, jnp.exp(ac_c), jnp.exp(ac_end - ac_c)], axis=0)
    full = _dot(stacked, expand, precision=HIGHEST)
    dt_full, dec_full, tail_full = full[:R], full[R:2 * R], full[2 * R:]
    xdt = x * dt_full
    xdt_b = _pad_rows(xdt.astype(BF16), CHUNK)
    xw = (xdt * tail_full).astype(BF16)

    for g in range(G):
        cg = cm[:, g * N:(g + 1) * N].astype(BF16)
        bg = bm[:, g * N:(g + 1) * N].astype(BF16)
        cb = _dot_nt(cg, _pad_rows(bg, CHUNK))
        gsl = slice(g * hg * P, (g + 1) * hg * P)
        y_inter = _seq_select(
            seq_c, nseq,
            lambda s: _dot_nt(cg, hs_ref[0, s, g * hg:(g + 1) * hg].reshape(hg * P, N).astype(BF16)))
        ys = []
        for hh in range(hg):
            h = g * hg + hh
            seg = jnp.exp(jnp.where(causal, ac_c[:, off + h:off + h + 1] - ac_r[off + h:off + h + 1, :], NEG))
            ys.append(_dot((cb * seg).astype(BF16), xdt_b[:, h * P:(h + 1) * P]))
        y_g = jnp.concatenate(ys, axis=1) + y_inter * dec_full[:, gsl]
        y_g = y_g + dsk_ref[:, gsl] * x[:, gsl]
        zf = z_ref[:, gsl].astype(F32)
        y_ref[:, gsl] = (y_g * (zf * _sigmoid(zf))).astype(y_ref.dtype)
        for s in range(nseq):
            xw_s = xw[:, gsl] if nseq == 1 else jnp.where(seq_c == s, xw[:, gsl], jnp.zeros((), BF16))
            upd = _dot_tn(xw_s, bg)
            for hh in range(hg):
                h = g * hg + hh
                dec = jnp.exp(ac_c[last[s]:last[s] + 1, off + h:off + h + 1])
                hs_ref[0, s, h] = dec * hs_ref[0, s, h] + upd[hh * P:(hh + 1) * P, :]


def _ssd(pc, pab, gates, gates_t, prev, conv_w, conv_b, dtb_r, dtb_c, al_r, al_c, dskip_full, h_all, layers,
         *, B, R, rps, nseq, nc, rb0, l_valid, zcol):
    T, CW = pc.shape
    nheads = h_all.shape[2]
    W = nheads * SSD_HEADDIM
    tile_rows = prev.shape[1]
    nb = B // nseq

    def rowmap(col):
        return lambda b, j: (rb0 + b * nc + j, col)

    def const(b, j):
        return (0, 0)

    body = functools.partial(_ssd_body, R=R, rps=rps, nseq=nseq, l_valid=l_valid, tile_rows=tile_rows)
    return _layered_state_call(
        body, (nb, nc),
        [pl.BlockSpec((R, CW), rowmap(0)), pl.BlockSpec((R, W), rowmap(zcol)),
         pl.BlockSpec((R, LANES), rowmap(0)),
         pl.BlockSpec((1, LANES, CHUNK), lambda b, j: (b * nc + j, 0, 0)),
         pl.BlockSpec((nseq, tile_rows, CW), lambda b, j: (b, 0, 0)),
         pl.BlockSpec((CONV_W, CW), const), pl.BlockSpec((1, CW), const),
         pl.BlockSpec((1, LANES), const), pl.BlockSpec((LANES, 1), const),
         pl.BlockSpec((1, LANES), const), pl.BlockSpec((LANES, 1), const),
         pl.BlockSpec((1, W), const)],
        [pc, pab, gates, gates_t, prev, conv_w, conv_b, dtb_r, dtb_c, al_r, al_c, dskip_full],
        [pl.BlockSpec((R, W), lambda b, j: (b * nc + j, 0))],
        [jax.ShapeDtypeStruct((nb * nc * R, W), BF16)],
        1, h_all, (nseq, nheads, SSD_HEADDIM, SSD_STATE), layers, "ssd",
        scratch_shapes=[pltpu.VMEM((nseq * tile_rows, CW), F32)])


def _lru_body(xl_ref, yl_ref, prev_ref, h0_ref, cw_ref, cb_ref, wa_ref, ba_ref, wi_ref, bi_ref,
              lam_ref, o_ref, hs_ref, prev_s, *, R, rps, nseq, l_valid, tile_rows):
    W = xl_ref.shape[1]
    bw = W // LRU_BLOCKS
    j = pl.program_id(1)

    @pl.when(j == 0)
    def _():
        hs_ref[...] = h0_ref[...]
        prev_s[...] = prev_ref[...].reshape(prev_s.shape)

    r, t, seq_c, seq_r, pos_c, pos_r = _row_ids(R, rps, R)
    valid_c = pos_c < l_valid
    u = xl_ref[...]
    taps = _conv_shift(u, prev_s[...], R, rps, tile_rows)
    if nseq == 1:
        prev_s[...] = u[R - tile_rows:, :].astype(F32)
    xl = cb_ref[...] + taps[0] * cw_ref[CONV_W - 1:CONV_W, :]
    for jj in range(1, CONV_W):
        xl = xl + taps[jj] * cw_ref[CONV_W - 1 - jj:CONV_W - jj, :]
    xb = xl.astype(BF16)
    ra = jnp.concatenate([_dot(xb[:, n * bw:(n + 1) * bw], wa_ref[n]) for n in range(LRU_BLOCKS)], axis=1)
    ri = jnp.concatenate([_dot(xb[:, n * bw:(n + 1) * bw], wi_ref[n]) for n in range(LRU_BLOCKS)], axis=1)
    rg = _sigmoid(ra + ba_ref[...])
    ig = _sigmoid(ri + bi_ref[...])
    log_a = -LRU_C * rg * _softplus(-lam_ref[...])
    a = jnp.where(valid_c, jnp.exp(log_a), 1.0)
    th = jnp.tanh(log_a)
    bv = jnp.where(valid_c, jnp.sqrt(-2.0 * th / (1.0 - th)) * (ig * xl), 0.0)
    i_in = r % rps if rps != R else r
    step = 1
    while step < rps:
        a_sh = pltpu.roll(a, step, 0)
        b_sh = pltpu.roll(bv, step, 0)
        m = i_in >= step
        bv = jnp.where(m, a * b_sh + bv, bv)
        a = jnp.where(m, a * a_sh, a)
        step *= 2
    h0 = _seq_select(seq_c, nseq, lambda s: hs_ref[s])
    h_seq = a * h0 + bv
    for s in range(nseq):
        lr = s * rps + rps - 1 if nseq > 1 else R - 1
        hs_ref[s] = h_seq[lr:lr + 1, :]
    o_ref[...] = (h_seq * _gelu_tanh(yl_ref[...].astype(F32))).astype(o_ref.dtype)


def _lru(pab, prev, h0, conv_w, conv_b, wa, ba, wi, bi, lam, *, B, R, rps, nseq, nc, rb0, l_valid,
         xcol, ycol):
    T = pab.shape[0]
    W = lam.shape[1]
    bw = W // LRU_BLOCKS
    tile_rows = prev.shape[1]
    nb = B // nseq

    def rowmap(col):
        return lambda b, j: (rb0 + b * nc + j, col)

    def const(b, j):
        return (0, 0)

    def const3(b, j):
        return (0, 0, 0)

    body = functools.partial(_lru_body, R=R, rps=rps, nseq=nseq, l_valid=l_valid, tile_rows=tile_rows)
    st = pl.BlockSpec((nseq, 1, W), lambda b, j: (b, 0, 0))
    vec = pl.BlockSpec((1, W), const)
    return pl.pallas_call(
        body, grid=(nb, nc),
        in_specs=[pl.BlockSpec((R, W), rowmap(xcol)), pl.BlockSpec((R, W), rowmap(ycol)),
                  pl.BlockSpec((nseq, tile_rows, W), lambda b, j: (b, 0, 0)), st,
                  pl.BlockSpec((CONV_W, W), const), vec,
                  pl.BlockSpec((LRU_BLOCKS, bw, bw), const3), vec,
                  pl.BlockSpec((LRU_BLOCKS, bw, bw), const3), vec, vec],
        out_specs=[pl.BlockSpec((R, W), lambda b, j: (b * nc + j, 0)), st],
        out_shape=[jax.ShapeDtypeStruct((nb * nc * R, W), BF16), jax.ShapeDtypeStruct(h0.shape, F32)],
        scratch_shapes=[pltpu.VMEM((nseq * tile_rows, W), F32)],
        compiler_params=_cparams(("parallel", "arbitrary")), name="lru",
    )(pab, pab, prev, h0, conv_w, conv_b, wa, ba, wi, bi, lam)


def _outproj_body(hmp_ref, hsp_ref, hlp_ref, hms_ref, hss_ref, hls_ref, x_ref, wm_ref, ws_ref, wl_ref,
                  gs_ref, g_ref, b_ref, wr_ref, br_ref, x1_ref, x1b_ref, idx_ref, gate_ref, *, alpha, npb):
    is_p = pl.program_id(0) < npb
    hm = jnp.where(is_p, hmp_ref[...], hms_ref[...])
    hl = jnp.where(is_p, hlp_ref[...], hls_ref[...])
    ysf = jnp.where(is_p, hsp_ref[...], hss_ref[...]).astype(F32)
    ysn = ysf * lax.rsqrt(jnp.mean(ysf * ysf, axis=1, keepdims=True) + RMS_EPS) * gs_ref[...]
    y = _dot(hm, wm_ref[...]) + _dot(ysn.astype(BF16), ws_ref[...]) + _dot(hl, wl_ref[...])
    x1 = _layer_norm(alpha * x_ref[...] + y, g_ref[...], b_ref[...])
    x1b = x1.astype(BF16)
    x1_ref[...] = x1
    x1b_ref[...] = x1b
    logits = _dot(x1b, wr_ref[...]) + br_ref[...]
    lane = lax.broadcasted_iota(jnp.int32, logits.shape, 1)
    idx_out = jnp.zeros(logits.shape, jnp.int32)
    val_out = jnp.zeros(logits.shape, F32)
    top = None
    for k in range(TOP_K):
        mx = jnp.max(logits, axis=1, keepdims=True)
        sel = jnp.min(jnp.where(logits == mx, lane, LANES), axis=1, keepdims=True)
        if top is None:
            top = mx
        idx_out = jnp.where(lane == k, sel, idx_out)
        val_out = jnp.where(lane == k, jnp.exp(mx - top), val_out)
        logits = jnp.where(lane == sel, -jnp.inf, logits)
    gate_ref[...] = val_out / jnp.sum(val_out, axis=1, keepdims=True)
    idx_ref[...] = idx_out


def _outproj(mix_p, mix_s, x, wm, ws, wl, g_ssd, ln_g, ln_b, wr, br, alpha):
    T, D = x.shape
    tm = OUT_TILE
    npb = mix_p[0].shape[0] // tm
    assert npb * tm == mix_p[0].shape[0] and mix_s[0].shape[0] % tm == 0

    def rows(w):
        return pl.BlockSpec((tm, w), lambda i: (i, 0))

    def rows_p(a):
        return pl.BlockSpec((tm, a.shape[1]), lambda i: (jnp.minimum(i, npb - 1), 0))

    def rows_s(a):
        return pl.BlockSpec((tm, a.shape[1]), lambda i: (jnp.maximum(i - npb, 0), 0))

    def whole(a):
        return pl.BlockSpec(a.shape, lambda i: (0, 0), pipeline_mode=pl.Buffered(1))

    return pl.pallas_call(
        functools.partial(_outproj_body, alpha=alpha, npb=npb), grid=(T // tm,),
        in_specs=[rows_p(mix_p[0]), rows_p(mix_p[1]), rows_p(mix_p[2]),
                  rows_s(mix_s[0]), rows_s(mix_s[1]), rows_s(mix_s[2]), rows(D),
                  whole(wm), whole(ws), whole(wl), whole(g_ssd), whole(ln_g), whole(ln_b),
                  whole(wr), whole(br)],
        out_specs=[rows(D), rows(D), rows(LANES), rows(LANES)],
        out_shape=[jax.ShapeDtypeStruct((T, D), F32), jax.ShapeDtypeStruct((T, D), BF16),
                   jax.ShapeDtypeStruct((T, LANES), jnp.int32), jax.ShapeDtypeStruct((T, LANES), F32)],
        compiler_params=_cparams(("parallel",)), name="outproj_ln_router",
    )(*mix_p, *mix_s, x, wm, ws, wl, g_ssd, ln_g, ln_b, wr, br)


def _expert_body(be_ref, na_ref, x_ref, wg_ref, bg_ref, wu_ref, bu_ref, wd_ref, bd_ref, o_ref):
    i = pl.program_id(0)
    j = pl.program_id(1)

    @pl.when(i < na_ref[0])
    def _():
        xb = x_ref[...]
        gate = jnp.minimum(_dot(xb, wg_ref[0, 0].astype(BF16)) + bg_ref[0, 0], SWIGLU_LIMIT)
        up = jnp.clip(_dot(xb, wu_ref[0, 0].astype(BF16)) + bu_ref[0, 0], -SWIGLU_LIMIT, SWIGLU_LIMIT)
        act = (up + 1.0) * gate * _sigmoid(SWIGLU_ALPHA * gate)
        contrib = _dot(act.astype(BF16), wd_ref[0, 0].astype(BF16))

        @pl.when(j == 0)
        def _():
            o_ref[...] = contrib + bd_ref[0, 0]

        @pl.when(j != 0)
        def _():
            o_ref[...] += contrib

    @pl.when((i >= na_ref[0]) & (j == 0))
    def _():
        o_ref[...] = jnp.zeros_like(o_ref)


def _experts(xs, blk_e, n_active, wg, bg, wu, bu, wd, bd, l):
    n_rows, D = xs.shape
    Fh = wg.shape[3]
    nblk = n_rows // MOE_TM
    nf = Fh // MOE_TF

    def jeff(i, j, na):
        return jnp.where(i < na[0], j, nf - 1)

    grid_spec = pltpu.PrefetchScalarGridSpec(
        num_scalar_prefetch=2, grid=(nblk, nf),
        in_specs=[pl.BlockSpec((MOE_TM, D), lambda i, j, be, na: (i, 0)),
                  pl.BlockSpec((1, 1, D, MOE_TF), lambda i, j, be, na: (l, be[i], 0, jeff(i, j, na))),
                  pl.BlockSpec((1, 1, 1, MOE_TF), lambda i, j, be, na: (l, be[i], 0, jeff(i, j, na))),
                  pl.BlockSpec((1, 1, D, MOE_TF), lambda i, j, be, na: (l, be[i], 0, jeff(i, j, na))),
                  pl.BlockSpec((1, 1, 1, MOE_TF), lambda i, j, be, na: (l, be[i], 0, jeff(i, j, na))),
                  pl.BlockSpec((1, 1, MOE_TF, D), lambda i, j, be, na: (l, be[i], jeff(i, j, na), 0)),
                  pl.BlockSpec((1, 1, 1, D), lambda i, j, be, na: (l, be[i], 0, 0))],
        out_specs=pl.BlockSpec((MOE_TM, D), lambda i, j, be, na: (i, 0)))
    return pl.pallas_call(
        _expert_body, grid_spec=grid_spec, out_shape=jax.ShapeDtypeStruct((n_rows, D), F32),
        compiler_params=_cparams(("parallel", "arbitrary")), name="moe_experts",
    )(blk_e, n_active, xs, wg, bg, wu, bu, wd, bd)


def _combine_body(yg_ref, gate_ref, x_ref, g_ref, b_ref, o_ref, ob_ref, *, alpha):
    gates = gate_ref[...]
    f = gates[:, 0:1] * yg_ref[0]
    for k in range(1, TOP_K):
        f = f + gates[:, k:k + 1] * yg_ref[k]
    x2 = _layer_norm(alpha * x_ref[...] + f, g_ref[...], b_ref[...])
    o_ref[...] = x2
    ob_ref[...] = x2.astype(BF16)


def _combine_ln(yg, gates, x, ln_g, ln_b, alpha):
    T, D = x.shape
    tm = OUT_TILE
    rows = pl.BlockSpec((tm, D), lambda i: (i, 0))
    vec = pl.BlockSpec((1, D), lambda i: (0, 0))
    return pl.pallas_call(
        functools.partial(_combine_body, alpha=alpha), grid=(T // tm,),
        in_specs=[pl.BlockSpec((TOP_K, tm, D), lambda i: (0, i, 0)),
                  pl.BlockSpec((tm, LANES), lambda i: (i, 0)), rows, vec, vec],
        out_specs=[rows, rows],
        out_shape=[jax.ShapeDtypeStruct((T, D), F32), jax.ShapeDtypeStruct((T, D), BF16)],
        compiler_params=_cparams(("parallel",)), name="moe_combine_ln",
    )(yg, gates, x, ln_g, ln_b)


def _moe_dispatch(idx, valid_tok, n_experts, n_rows):
    T = idx.shape[0]
    flat_e = jnp.where(valid_tok[:, None], idx[:, :TOP_K], n_experts).reshape(-1)
    order = jnp.argsort(flat_e)
    sorted_e = flat_e[order]
    counts = jnp.bincount(flat_e, length=n_experts + 1)[:n_experts]
    padded = (counts + MOE_TM - 1) // MOE_TM * MOE_TM
    pad_end = jnp.cumsum(padded)
    pad_start = pad_end - padded
    start = jnp.cumsum(counts) - counts
    e_clip = jnp.minimum(sorted_e, n_experts - 1)
    dest_sorted = pad_start[e_clip] + jnp.arange(T * TOP_K) - start[e_clip]
    dest_sorted = jnp.minimum(dest_sorted, n_rows - 1).astype(jnp.int32)
    nblk = n_rows // MOE_TM
    blk_e = jnp.searchsorted(pad_end, jnp.arange(nblk) * MOE_TM, side="right")
    n_active = (pad_end[-1] // MOE_TM).astype(jnp.int32)
    last_e = jnp.minimum(blk_e[jnp.maximum(n_active - 1, 0)], n_experts - 1)
    blk_e = jnp.where(jnp.arange(nblk) < n_active, jnp.minimum(blk_e, n_experts - 1), last_e).astype(jnp.int32)
    row_e = jnp.repeat(blk_e, MOE_TM)
    off = jnp.arange(n_rows) - pad_start[row_e]
    src = jnp.where(off < counts[row_e], start[row_e] + off, 0)
    row_tok = (order[src] // TOP_K).astype(jnp.int32)
    inv = jnp.argsort(order)
    dest = dest_sorted[inv]
    return row_tok, blk_e, n_active.reshape(1), dest.reshape(T, TOP_K)


def kernel(x_prompt, x_sample, state_mlstm_C, state_mlstm_n, state_mlstm_m, state_ssd_h, state_ssd_conv,
           state_lru_h, state_lru_conv, meta, ln_in_g, ln_in_b, w_in, b_if, conv_ssd_w, conv_ssd_b, dt_bias,
           a_log, d_skip, g_ssd, g_mlstm, conv_lru_w, conv_lru_b, w_rg_a, b_rg_a, w_rg_i, b_rg_i, lam, w_out,
           ln1_g, ln1_b, w_router, b_router, w_gate, b_gate, w_up, b_up, w_down, b_down, ln2_g, ln2_b):
    Bp, S, D = x_prompt.shape
    Bs, Ls, _ = x_sample.shape
    depth = w_in.shape[0]
    n_meta = meta.shape[0]
    nh = MLSTM_HEADS
    MW = g_mlstm.shape[1]
    SW = g_ssd.shape[1]
    LW = lam.shape[1]
    CW = conv_ssd_w.shape[2]
    n_ssd_heads = dt_bias.shape[1]
    E = w_router.shape[2]
    alpha = (2 * depth) ** 0.25
    Lp = n_meta + S
    LP = -(-Lp // CHUNK) * CHUNK
    LS = SAMPLE_LS
    Tp, Ts = Bp * LP, Bs * LS
    T = Tp + Ts
    ncp = LP // CHUNK
    RS = SAMPLE_SEQS * LS
    assert CONV_W - 1 <= Ls <= LS and T % ROW_TILE == 0 and T % OUT_TILE == 0 and Bs % SAMPLE_SEQS == 0 and Tp % RS == 0

    xp = jnp.concatenate([jnp.broadcast_to(meta.astype(F32)[None], (Bp, n_meta, D)), x_prompt,
                          jnp.zeros((Bp, LP - Lp, D), F32)], axis=1).reshape(Tp, D)
    xs = jnp.concatenate([x_sample, jnp.zeros((Bs, LS - Ls, D), F32)], axis=1).reshape(Ts, D)
    x, xb = _ln_in(jnp.concatenate([xp, xs], axis=0), ln_in_g, ln_in_b)
    pos = np.concatenate([np.tile(np.arange(LP) < Lp, Bp), np.tile(np.arange(LS) < Ls, Bs)])
    valid_tok = jnp.asarray(pos)

    sizes = [MW, MW, MW, MW, nh, nh, SW, CW, n_ssd_heads, LW, LW]
    o = [0] + [int(v) for v in np.cumsum(sizes)]
    w_ab = jnp.concatenate([w_in[:, :, o[0]:o[4]], w_in[:, :, o[6]:o[7]], w_in[:, :, o[9]:o[11]]],
                           axis=2).astype(BF16)
    w_c = w_in[:, :, o[7]:o[8]].astype(BF16)
    n_gate = 2 * nh + n_ssd_heads
    w_d = jnp.concatenate([w_in[:, :, o[4]:o[6]], w_in[:, :, o[8]:o[9]],
                           jnp.zeros((depth, D, LANES - n_gate), F32)], axis=2).astype(BF16)
    zcol = (4 * MW) // SW
    xcol = (4 * MW + SW) // LW
    ycol = xcol + 1
    assert zcol * SW == 4 * MW and xcol * LW == 4 * MW + SW

    def lane_pad(v, fill=0.0, front=0):
        return jnp.concatenate([jnp.full((front,), fill, F32), v.astype(F32),
                                jnp.full((LANES - front - v.shape[0],), fill, F32)])

    w_out_b = w_out.astype(BF16)
    wr_b = jnp.concatenate([w_router, jnp.zeros((depth, D, LANES - E), F32)], axis=2).astype(BF16)
    n_assign = (Bp * Lp + Bs * Ls) * TOP_K
    n_rows = -(-(n_assign + E * (MOE_TM - 1)) // MOE_TM) * MOE_TM

    zeros_p = dict(
        c=jnp.zeros((1, Bp, nh, MW // nh, MW // nh), F32), n=jnp.zeros((Bp, nh, MW // nh), F32),
        m=jnp.zeros((Bp, 1, nh), F32), h=jnp.zeros((1, Bp, n_ssd_heads, SSD_HEADDIM, SSD_STATE), F32),
        cs=jnp.zeros((Bp, 16, CW), F32), l=jnp.zeros((Bp, 1, LW), F32), cl=jnp.zeros((Bp, 16, LW), F32))
    pgrp = dict(B=Bp, R=CHUNK, rps=CHUNK, nseq=1, nc=ncp, rb0=0, l_valid=Lp)
    sgrp = dict(B=Bs, R=RS, rps=LS, nseq=SAMPLE_SEQS, nc=1, rb0=Tp // RS, l_valid=Ls)

    outs = [[] for _ in range(14)]
    bg4, bu4, bd4 = (b.reshape(depth, E, 1, b.shape[2]) for b in (b_gate, b_up, b_down))
    s_c_all = s_h_all = None
    for l in range(depth):
        pab = _matmul(xb, w_ab, l, BF16, 1024, "inproj_ab")
        pc = _matmul(xb, w_c, l, BF16, CW // 2, "inproj_c")
        gates = _matmul(xb, w_d, l, F32, LANES, "inproj_gates")
        gt_p = gates[:Tp].reshape(Bp * ncp, CHUNK, LANES).transpose(0, 2, 1)
        gt_s = gates[Tp:].reshape(Bs // SAMPLE_SEQS, RS, LANES).transpose(0, 2, 1)
        gt_s = jnp.concatenate([gt_s, jnp.zeros((Bs // SAMPLE_SEQS, LANES, CHUNK - RS), F32)], axis=2)

        bif_r = lane_pad(b_if[l]).reshape(1, LANES)
        bif_c = bif_r.reshape(LANES, 1)
        dtb_r = lane_pad(dt_bias[l], front=2 * nh).reshape(1, LANES)
        al_r = lane_pad(a_log[l], front=2 * nh).reshape(1, LANES)
        dsk = jnp.repeat(d_skip[l].astype(F32), SSD_HEADDIM).reshape(1, SW)
        gm = g_mlstm[l].reshape(1, MW).astype(F32)
        cw_s, cb_s = conv_ssd_w[l].astype(F32), conv_ssd_b[l].reshape(1, CW).astype(F32)
        cw_l, cb_l = conv_lru_w[l].astype(F32), conv_lru_b[l].reshape(1, LW).astype(F32)
        wa, wi = w_rg_a[l].astype(BF16), w_rg_i[l].astype(BF16)
        ba, bi = b_rg_a[l].reshape(1, LW).astype(F32), b_rg_i[l].reshape(1, LW).astype(F32)
        lam_l = lam[l].reshape(1, LW).astype(F32)

        def pad_conv(buf):
            return jnp.concatenate([jnp.zeros((buf.shape[0], SUBLANES - (CONV_W - 1), buf.shape[2]), F32),
                                    buf.astype(F32)], axis=1)

        st_s = dict(c=state_mlstm_C.astype(F32), n=state_mlstm_n[l].astype(F32),
                    m=state_mlstm_m[l].astype(F32).reshape(Bs, 1, nh), h=state_ssd_h.astype(F32),
                    cs=pad_conv(state_ssd_conv[l]), l=state_lru_h[l].astype(F32).reshape(Bs, 1, LW),
                    cl=pad_conv(state_lru_conv[l]))

        mixes, new_states = [], []
        for gi, (grp, st, gt) in enumerate(((pgrp, zeros_p, gt_p), (sgrp, st_s, gt_s))):
            lay_c = (0, 0, 1, None) if gi == 0 else (l, l, depth, s_c_all)
            lay_h = (0, 0, 1, None) if gi == 0 else (l, l, depth, s_h_all)
            hm_g, c_n, n_n, m_n = _mlstm(pab, gates, gt, bif_r, bif_c, gm, st["n"], st["m"], st["c"], lay_c, **grp)
            hs_g, h_n = _ssd(pc, pab, gates, gt, st["cs"], cw_s, cb_s, dtb_r, dtb_r.reshape(LANES, 1),
                             al_r, al_r.reshape(LANES, 1), dsk, st["h"], lay_h, zcol=zcol, **grp)
            if gi == 1:
                s_c_all, s_h_all = c_n, h_n
            hl_g, l_n = _lru(pab, st["cl"], st["l"], cw_l, cb_l, wa, ba, wi, bi, lam_l, xcol=xcol, ycol=ycol,
                             **grp)
            new_states.append((c_n, n_n, m_n, h_n, l_n))
            mixes.append((hm_g, hs_g, hl_g))

        x1, x1b, idx, gate_w = _outproj(
            mixes[0], mixes[1], x, w_out_b[l, :MW], w_out_b[l, MW:MW + SW], w_out_b[l, MW + SW:],
            g_ssd[l].reshape(1, SW).astype(F32), ln1_g[l].reshape(1, D), ln1_b[l].reshape(1, D),
            wr_b[l], lane_pad(b_router[l], fill=NEG).reshape(1, LANES), alpha)

        row_tok, blk_e, n_active, dest = _moe_dispatch(idx, valid_tok, E, n_rows)
        yb = _experts(x1b[row_tok], blk_e, n_active, w_gate, bg4, w_up, bu4, w_down, bd4, l)
        yg = yb[dest.T]
        x, xb = _combine_ln(yg, gate_w, x1, ln2_g[l].reshape(1, D), ln2_b[l].reshape(1, D), alpha)

        xbc_p = pc[:Tp].reshape(Bp, LP, CW)[:, Lp - (CONV_W - 1):Lp].astype(F32)
        xbc_s = pc[Tp:].reshape(Bs, LS, CW)[:, Ls - (CONV_W - 1):Ls].astype(F32)
        xl_p = pab[:Tp, xcol * LW:(xcol + 1) * LW].reshape(Bp, LP, LW)[:, Lp - (CONV_W - 1):Lp].astype(F32)
        xl_s = pab[Tp:, xcol * LW:(xcol + 1) * LW].reshape(Bs, LS, LW)[:, Ls - (CONV_W - 1):Ls].astype(F32)
        for gi, (bsz, conv_s, conv_l) in enumerate(((Bp, xbc_p, xl_p), (Bs, xbc_s, xl_s))):
            c_n, n_n, m_n, h_n, l_n = new_states[gi]
            vals = (c_n[0], n_n, m_n.reshape(bsz, nh), h_n[0], conv_s, l_n.reshape(bsz, LW), conv_l)
            for k, v in enumerate(vals):
                if gi == 0 or k not in (0, 3):
                    outs[gi * 7 + k].append(v)

    y_prompt = x[:Tp].reshape(Bp, LP, D)[:, n_meta:Lp]
    y_sample = x[Tp:].reshape(Bs, LS, D)[:, :Ls]
    state_dtypes = [state_mlstm_C.dtype, state_mlstm_n.dtype, state_mlstm_m.dtype, state_ssd_h.dtype,
                    state_ssd_conv.dtype, state_lru_h.dtype, state_lru_conv.dtype]
    stacked = [jnp.stack(v).astype(state_dtypes[k % 7]) if v else None for k, v in enumerate(outs)]
    stacked[7] = s_c_all.astype(state_dtypes[0])
    stacked[7 + 3] = s_h_all.astype(state_dtypes[3])
    return (y_prompt, y_sample, *stacked)
```

```python
import functools
import math

import numpy as np
import jax
import jax.numpy as jnp
from jax import lax
from jax.experimental import pallas as pl
from jax.experimental.pallas import tpu as pltpu

F32 = jnp.float32
BF16 = jnp.bfloat16

LANES = 128
SUBLANES = 8
VMEM_LIMIT = 56 * 1024 * 1024

CONV_W = 4
MLSTM_HEADS = 4
SSD_HEADDIM = 64
SSD_STATE = 128
SSD_GROUPS = 2
LRU_BLOCKS = 8
LRU_C = 8.0
TOP_K = 4
SWIGLU_LIMIT = 7.0
SWIGLU_ALPHA = 1.702
LN_EPS = 1e-5
RMS_EPS = 1e-6
NEG = -1e30

CHUNK = 128
SAMPLE_LS = 8
SAMPLE_SEQS = 4
ROW_TILE = 512
MOE_SUB = 256
MOE_NSUB = 5
MOE_TF = 256
DMA_UNROLL = 8
OUT_TILE = 256


def _cparams(sem):
    return pltpu.CompilerParams(dimension_semantics=sem, vmem_limit_bytes=VMEM_LIMIT)


def _softplus(x):
    return jnp.maximum(x, 0.0) + jnp.log1p(jnp.exp(-jnp.abs(x)))


def _log_sigmoid(x):
    return -_softplus(-x)


def _sigmoid(x):
    return 1.0 / (1.0 + jnp.exp(-x))


def _gelu_tanh(x):
    return 0.5 * x * (1.0 + jnp.tanh(math.sqrt(2.0 / math.pi) * (x + 0.044715 * (x * x * x))))


def _layer_norm(x, g, b):
    mu = jnp.mean(x, axis=-1, keepdims=True)
    xc = x - mu
    var = jnp.mean(xc * xc, axis=-1, keepdims=True)
    return xc * lax.rsqrt(var + LN_EPS) * g + b


def _dot(a, b, precision=None):
    return jnp.dot(a, b, preferred_element_type=F32, precision=precision)


def _split_bf16(x):
    hi = x.astype(BF16)
    return hi, (x - hi.astype(F32)).astype(BF16)


def _dot_sel(x, sel):
    hi, lo = _split_bf16(x)
    return _dot(hi, sel) + _dot(lo, sel)


def _sel_dot(sel, x):
    hi, lo = _split_bf16(x)
    return _dot(sel, hi) + _dot(sel, lo)


def _dot_nt(a, b):
    return lax.dot_general(a, b, (((1,), (1,)), ((), ())), preferred_element_type=F32)


def _dot_tn(a, b):
    return lax.dot_general(a, b, (((0,), (0,)), ((), ())), preferred_element_type=F32)


def _ln_in_body(x_ref, g_ref, b_ref, o_ref, ob_ref):
    y = _layer_norm(x_ref[...], g_ref[...], b_ref[...])
    o_ref[...] = y
    ob_ref[...] = y.astype(BF16)


def _ln_in(x, g, b):
    T, D = x.shape
    row = pl.BlockSpec((ROW_TILE, D), lambda i: (i, 0))
    vec = pl.BlockSpec((1, D), lambda i: (0, 0))
    return pl.pallas_call(
        _ln_in_body, grid=(T // ROW_TILE,), in_specs=[row, vec, vec], out_specs=[row, row],
        out_shape=[jax.ShapeDtypeStruct((T, D), F32), jax.ShapeDtypeStruct((T, D), BF16)],
        compiler_params=_cparams(("parallel",)), name="ln_in")(x, g.reshape(1, D), b.reshape(1, D))


def _mm_body(x_ref, w_ref, o_ref):
    o_ref[...] = _dot(x_ref[...], w_ref[0]).astype(o_ref.dtype)


def _matmul(x, w, l, out_dtype, tn, name):
    T, K = x.shape
    N = w.shape[2]
    return pl.pallas_call(
        _mm_body, grid=(N // tn, T // ROW_TILE),
        in_specs=[pl.BlockSpec((ROW_TILE, K), lambda j, i: (i, 0)),
                  pl.BlockSpec((1, K, tn), lambda j, i: (l, 0, j))],
        out_specs=pl.BlockSpec((ROW_TILE, tn), lambda j, i: (i, j)),
        out_shape=jax.ShapeDtypeStruct((T, N), out_dtype),
        compiler_params=_cparams(("parallel", "parallel")), name=name)(x, w)


def _row_ids(R, rps, chunk_rows):
    r = lax.broadcasted_iota(jnp.int32, (R, 1), 0)
    t = lax.broadcasted_iota(jnp.int32, (1, CHUNK), 1)
    if rps == R:
        j = pl.program_id(1)
        return r, t, jnp.zeros_like(r), jnp.zeros_like(t), r + j * chunk_rows, t + j * chunk_rows
    return r, t, r // rps, t // rps, r % rps, t % rps


def _conv_shift(u_bf, prev_f32, R, rps, tile_rows):
    P = prev_f32.shape[0]
    r = lax.broadcasted_iota(jnp.int32, (R, 1), 0)
    ku = lax.broadcasted_iota(jnp.int32, (1, R), 1)
    kp = lax.broadcasted_iota(jnp.int32, (1, P), 1)
    i = r % rps
    seq = r // rps
    taps = [u_bf.astype(F32)]
    for j in range(1, CONV_W):
        su = ((i >= j) & (ku == r - j)).astype(BF16)
        sp = ((i < j) & (kp == seq * tile_rows + (tile_rows - j) + i)).astype(BF16)
        taps.append(_dot(su, u_bf) + _sel_dot(sp, prev_f32))
    return taps


def _seq_select(seq_c, nseq, fn):
    if nseq == 1:
        return fn(0)
    out = None
    for s in range(nseq):
        term = jnp.where(seq_c == s, fn(s), 0.0)
        out = term if out is None else out + term
    return out


def _pad_rows(x, rows):
    if x.shape[0] == rows:
        return x
    return jnp.concatenate([x, jnp.zeros((rows - x.shape[0],) + x.shape[1:], x.dtype)], axis=0)


def _mlstm_body(q_ref, k_ref, v_ref, o_ref, g_ref, gt_ref, bifr_ref, bifc_ref, gm_ref,
                n0_ref, m0_ref, c0_ref, h_ref, c_ref, n_ref, m_ref, *, R, rps, nseq, l_valid):
    nh = MLSTM_HEADS
    dh = q_ref.shape[1] // nh
    j = pl.program_id(1)

    @pl.when(j == 0)
    def _():
        c_ref[...] = c0_ref[...]
        n_ref[...] = n0_ref[...]
        m_ref[...] = m0_ref[...]

    r, t, seq_c, seq_r, pos_c, pos_r = _row_ids(R, rps, R)
    valid_c = pos_c < l_valid
    valid_r = (t < R) & (pos_r < l_valid)
    same = (seq_c == seq_r) & (t < R)
    causal = same & (t <= r)

    pre_c = g_ref[...] + bifr_ref[...]
    pre_r = gt_ref[0] + bifc_ref[...]
    f_c = jnp.where(valid_c, _log_sigmoid(pre_c), 0.0)
    f_r = jnp.where(valid_r, _log_sigmoid(pre_r), 0.0)
    i_c = jnp.where(valid_c, pre_c, NEG)
    i_r = jnp.where(valid_r, pre_r, NEG)
    rq = lax.broadcasted_iota(jnp.int32, (1, R), 1)
    lower = (((r // rps) == (rq // rps)) & (rq <= r)).astype(BF16) if rps != R else (rq <= r).astype(BF16)
    b_c = _sel_dot(lower, f_c)
    tr = lax.broadcasted_iota(jnp.int32, (CHUNK, 1), 0)
    upper = ((tr <= t) & (tr < R) & ((tr // rps) == (t // rps))).astype(BF16)
    b_r = _dot_sel(f_r, upper)

    lane_h = lax.broadcasted_iota(jnp.int32, (1, nh), 1)
    m_old = [m_ref[s] for s in range(nseq)]
    m_new_rows = [jnp.zeros((1, nh), F32) for _ in range(nseq)]

    for h in range(nh):
        sl = slice(h * dh, (h + 1) * dh)
        q = q_ref[:, sl]
        ks = k_ref[:, sl] * jnp.asarray(dh ** -0.5, BF16)
        v = v_ref[:, sl]
        bc = b_c[:, nh + h:nh + h + 1]
        br = b_r[nh + h:nh + h + 1, :]
        ic = i_c[:, h:h + 1]
        ir = i_r[h:h + 1, :]
        log_d = jnp.where(causal, bc - br + ir, NEG)
        mm = _seq_select(seq_c, nseq, lambda s: m_old[s][:, h:h + 1])
        log_p = bc + mm
        m_t = jnp.maximum(log_p, jnp.max(log_d, axis=1, keepdims=True))
        d = jnp.exp(log_d - m_t)
        wp = jnp.exp(log_p - m_t)
        s_mat = _dot_nt(q, _pad_rows(ks, CHUNK)) * d
        qf = q.astype(F32)
        num_state = _seq_select(seq_c, nseq, lambda s: _dot(q, c_ref[0, s, h].astype(BF16)))
        den_state = _seq_select(
            seq_c, nseq, lambda s: jnp.sum(qf * n_ref[s, h:h + 1, :], axis=1, keepdims=True))
        num = _dot(s_mat.astype(BF16), _pad_rows(v, CHUNK)) + wp * num_state
        den = jnp.sum(s_mat, axis=1, keepdims=True) + wp * den_state
        hh = num / jnp.maximum(jnp.abs(den), jnp.exp(-m_t))
        hn = hh * lax.rsqrt(jnp.mean(hh * hh, axis=1, keepdims=True) + RMS_EPS) * gm_ref[:, sl]
        h_ref[:, sl] = (hn * _sigmoid(o_ref[:, sl].astype(F32))).astype(h_ref.dtype)

        last = [s * rps + rps - 1 if nseq > 1 else R - 1 for s in range(nseq)]
        b_end = _seq_select(seq_c, nseq, lambda s: bc[last[s]:last[s] + 1, :])
        m_end = _seq_select(seq_c, nseq, lambda s: m_t[last[s]:last[s] + 1, :])
        w_s = jnp.exp(b_end - bc + ic - m_end)
        kw = ks.astype(F32) * w_s
        for s in range(nseq):
            be = bc[last[s]:last[s] + 1, :]
            mn = m_t[last[s]:last[s] + 1, :]
            w_prev = jnp.exp(be + m_old[s][:, h:h + 1] - mn)
            kw_s = kw if nseq == 1 else jnp.where(seq_c == s, kw, 0.0)
            c_ref[0, s, h] = w_prev * c_ref[0, s, h] + _dot_tn(kw_s.astype(BF16), v)
            n_ref[s, h:h + 1, :] = w_prev * n_ref[s, h:h + 1, :] + jnp.sum(kw_s, axis=0, keepdims=True)
            m_new_rows[s] = m_new_rows[s] + jnp.where(lane_h == h, mn, 0.0)

    for s in range(nseq):
        m_ref[s] = m_new_rows[s]


def _drop_alias(body, n_in):
    def wrapped(*refs):
        return body(*refs[:n_in], *refs[n_in + 1:])
    return wrapped


def _layered_state_call(body, grid, in_specs, args, out_specs, out_shapes, big_out, state_in, state_blk,
                        layers, name, scratch_shapes=()):
    l_in, l_out, n_out_layers, prev_out = layers
    zeros = (0,) * (len(state_blk) - 1)
    in_specs = list(in_specs) + [pl.BlockSpec((1,) + state_blk, lambda b, j: (l_in, b) + zeros)]
    args = list(args) + [state_in]
    out_specs = list(out_specs)
    out_specs.insert(big_out, pl.BlockSpec((1,) + state_blk, lambda b, j: (l_out, b) + zeros))
    out_shapes = list(out_shapes)
    out_shapes.insert(big_out, jax.ShapeDtypeStruct((n_out_layers,) + state_in.shape[1:], F32))
    aliases = {}
    if prev_out is not None:
        body = _drop_alias(body, len(args))
        aliases = {len(args): big_out}
        in_specs.append(pl.BlockSpec(memory_space=pl.ANY))
        args.append(prev_out)
    return pl.pallas_call(
        body, grid=grid, in_specs=in_specs, out_specs=out_specs, out_shape=out_shapes,
        input_output_aliases=aliases, scratch_shapes=list(scratch_shapes),
        compiler_params=_cparams(("parallel", "arbitrary")), name=name)(*args)


def _mlstm(pab, gates, gates_t, bif_r, bif_c, g_m, n0, m0, c_all, layers, *, B, R, rps, nseq, nc, rb0,
           l_valid):
    nh = MLSTM_HEADS
    W = g_m.shape[1]
    dh = W // nh

    def rowmap(col):
        return lambda b, j: (rb0 + b * nc + j, col)

    def const(b, j):
        return (0, 0)

    nb = B // nseq
    body = functools.partial(_mlstm_body, R=R, rps=rps, nseq=nseq, l_valid=l_valid)
    st3 = pl.BlockSpec((nseq, nh, dh), lambda b, j: (b, 0, 0))
    stm = pl.BlockSpec((nseq, 1, nh), lambda b, j: (b, 0, 0))
    return _layered_state_call(
        body, (nb, nc),
        [pl.BlockSpec((R, W), rowmap(0)), pl.BlockSpec((R, W), rowmap(1)),
         pl.BlockSpec((R, W), rowmap(2)), pl.BlockSpec((R, W), rowmap(3)),
         pl.BlockSpec((R, LANES), rowmap(0)),
         pl.BlockSpec((1, LANES, CHUNK), lambda b, j: (b * nc + j, 0, 0)),
         pl.BlockSpec((1, LANES), const), pl.BlockSpec((LANES, 1), const),
         pl.BlockSpec((1, W), const), st3, stm],
        [pab, pab, pab, pab, gates, gates_t, bif_r, bif_c, g_m, n0, m0],
        [pl.BlockSpec((R, W), lambda b, j: (b * nc + j, 0)), st3, stm],
        [jax.ShapeDtypeStruct((nb * nc * R, W), BF16), jax.ShapeDtypeStruct(n0.shape, F32),
         jax.ShapeDtypeStruct(m0.shape, F32)],
        1, c_all, (nseq, nh, dh, dh), layers, "mlstm")


def _ssd_body(xbc_ref, z_ref, g_ref, gt_ref, prev_ref, cw_ref, cb_ref, dtbr_ref, dtbc_ref,
              alr_ref, alc_ref, dsk_ref, h0_ref, y_ref, hs_ref, prev_s, *, R, rps, nseq, l_valid, tile_rows):
    nheads = h0_ref.shape[2]
    P = SSD_HEADDIM
    N = SSD_STATE
    G = SSD_GROUPS
    hg = nheads // G
    W = nheads * P
    j = pl.program_id(1)

    @pl.when(j == 0)
    def _():
        hs_ref[...] = h0_ref[...]
        prev_s[...] = prev_ref[...].reshape(prev_s.shape)

    r, t, seq_c, seq_r, pos_c, pos_r = _row_ids(R, rps, R)
    valid_c = pos_c < l_valid
    valid_r = (t < R) & (pos_r < l_valid)
    causal = (seq_c == seq_r) & (t < R) & (t <= r)

    u = xbc_ref[...]
    taps = _conv_shift(u, prev_s[...], R, rps, tile_rows)
    if nseq == 1:
        prev_s[...] = u[R - tile_rows:, :].astype(F32)
    acc = cb_ref[...] + taps[0] * cw_ref[CONV_W - 1:CONV_W, :]
    for jj in range(1, CONV_W):
        acc = acc + taps[jj] * cw_ref[CONV_W - 1 - jj:CONV_W - jj, :]
    xc = acc * _sigmoid(acc)
    x = xc[:, :W]
    bm = xc[:, W:W + G * N]
    cm = xc[:, W + G * N:]

    off = 2 * MLSTM_HEADS
    dt_c = jnp.where(valid_c, _softplus(g_ref[...] + dtbr_ref[...]), 0.0)
    dt_r = jnp.where(valid_r, _softplus(gt_ref[0] + dtbc_ref[...]), 0.0)
    adt_c = dt_c * (-jnp.exp(alr_ref[...]))
    adt_r = dt_r * (-jnp.exp(alc_ref[...]))
    rq = lax.broadcasted_iota(jnp.int32, (1, R), 1)
    lower = (((r // rps) == (rq // rps)) & (rq <= r)).astype(BF16) if rps != R else (rq <= r).astype(BF16)
    ac_c = _sel_dot(lower, adt_c)
    tr = lax.broadcasted_iota(jnp.int32, (CHUNK, 1), 0)
    upper = ((tr <= t) & (tr < R) & ((tr // rps) == (t // rps))).astype(BF16)
    ac_r = _dot_sel(adt_r, upper)

    last = [s * rps + rps - 1 if nseq > 1 else R - 1 for s in range(nseq)]
    ac_end = _seq_select(seq_c, nseq, lambda s: ac_c[last[s]:last[s] + 1, :])
    eh = lax.broadcasted_iota(jnp.int32, (LANES, W), 0)
    el = lax.broadcasted_iota(jnp.int32, (LANES, W), 1)
    expand = (eh == off + el // P).astype(BF16)
    stacked = jnp.concatenate([dt_c, jnp.exp(ac_c), jnp.exp(ac_end - ac_c)], axis=0)
    full = _dot_sel(stacked, expand)
    dt_full, dec_full, tail_full = full[:R], full[R:2 * R], full[2 * R:]
    xdt = x * dt_full
    xdt_b = _pad_rows(xdt.astype(BF16), CHUNK)
    xw = (xdt * tail_full).astype(BF16)

    for g in range(G):
        cg = cm[:, g * N:(g + 1) * N].astype(BF16)
        bg = bm[:, g * N:(g + 1) * N].astype(BF16)
        cb = _dot_nt(cg, _pad_rows(bg, CHUNK))
        gsl = slice(g * hg * P, (g + 1) * hg * P)
        y_inter = _seq_select(
            seq_c, nseq,
            lambda s: _dot_nt(cg, hs_ref[0, s, g * hg:(g + 1) * hg].reshape(hg * P, N).astype(BF16)))
        ys = []
        for hh in range(hg):
            h = g * hg + hh
            seg = jnp.exp(jnp.where(causal, ac_c[:, off + h:off + h + 1] - ac_r[off + h:off + h + 1, :], NEG))
            ys.append(_dot((cb * seg).astype(BF16), xdt_b[:, h * P:(h + 1) * P]))
        y_g = jnp.concatenate(ys, axis=1) + y_inter * dec_full[:, gsl]
        y_g = y_g + dsk_ref[:, gsl] * x[:, gsl]
        zf = z_ref[:, gsl].astype(F32)
        y_ref[:, gsl] = (y_g * (zf * _sigmoid(zf))).astype(y_ref.dtype)
        for s in range(nseq):
            xw_s = xw[:, gsl] if nseq == 1 else jnp.where(seq_c == s, xw[:, gsl], jnp.zeros((), BF16))
            upd = _dot_tn(xw_s, bg)
            for hh in range(hg):
                h = g * hg + hh
                dec = jnp.exp(ac_c[last[s]:last[s] + 1, off + h:off + h + 1])
                hs_ref[0, s, h] = dec * hs_ref[0, s, h] + upd[hh * P:(hh + 1) * P, :]


def _ssd(pc, pab, gates, gates_t, prev, conv_w, conv_b, dtb_r, dtb_c, al_r, al_c, dskip_full, h_all, layers,
         *, B, R, rps, nseq, nc, rb0, l_valid, zcol):
    T, CW = pc.shape
    nheads = h_all.shape[2]
    W = nheads * SSD_HEADDIM
    tile_rows = prev.shape[1]
    nb = B // nseq

    def rowmap(col):
        return lambda b, j: (rb0 + b * nc + j, col)

    def const(b, j):
        return (0, 0)

    body = functools.partial(_ssd_body, R=R, rps=rps, nseq=nseq, l_valid=l_valid, tile_rows=tile_rows)
    return _layered_state_call(
        body, (nb, nc),
        [pl.BlockSpec((R, CW), rowmap(0)), pl.BlockSpec((R, W), rowmap(zcol)),
         pl.BlockSpec((R, LANES), rowmap(0)),
         pl.BlockSpec((1, LANES, CHUNK), lambda b, j: (b * nc + j, 0, 0)),
         pl.BlockSpec((nseq, tile_rows, CW), lambda b, j: (b, 0, 0)),
         pl.BlockSpec((CONV_W, CW), const), pl.BlockSpec((1, CW), const),
         pl.BlockSpec((1, LANES), const), pl.BlockSpec((LANES, 1), const),
         pl.BlockSpec((1, LANES), const), pl.BlockSpec((LANES, 1), const),
         pl.BlockSpec((1, W), const)],
        [pc, pab, gates, gates_t, prev, conv_w, conv_b, dtb_r, dtb_c, al_r, al_c, dskip_full],
        [pl.BlockSpec((R, W), lambda b, j: (b * nc + j, 0))],
        [jax.ShapeDtypeStruct((nb * nc * R, W), BF16)],
        1, h_all, (nseq, nheads, SSD_HEADDIM, SSD_STATE), layers, "ssd",
        scratch_shapes=[pltpu.VMEM((nseq * tile_rows, CW), F32)])


def _lru_body(xl_ref, yl_ref, prev_ref, h0_ref, cw_ref, cb_ref, wa_ref, ba_ref, wi_ref, bi_ref,
              lam_ref, o_ref, hs_ref, prev_s, *, R, rps, nseq, l_valid, tile_rows):
    W = xl_ref.shape[1]
    bw = W // LRU_BLOCKS
    j = pl.program_id(1)

    @pl.when(j == 0)
    def _():
        hs_ref[...] = h0_ref[...]
        prev_s[...] = prev_ref[...].reshape(prev_s.shape)

    r, t, seq_c, seq_r, pos_c, pos_r = _row_ids(R, rps, R)
    valid_c = pos_c < l_valid
    u = xl_ref[...]
    taps = _conv_shift(u, prev_s[...], R, rps, tile_rows)
    if nseq == 1:
        prev_s[...] = u[R - tile_rows:, :].astype(F32)
    xl = cb_ref[...] + taps[0] * cw_ref[CONV_W - 1:CONV_W, :]
    for jj in range(1, CONV_W):
        xl = xl + taps[jj] * cw_ref[CONV_W - 1 - jj:CONV_W - jj, :]
    xb = xl.astype(BF16)
    ra = jnp.concatenate([_dot(xb[:, n * bw:(n + 1) * bw], wa_ref[n]) for n in range(LRU_BLOCKS)], axis=1)
    ri = jnp.concatenate([_dot(xb[:, n * bw:(n + 1) * bw], wi_ref[n]) for n in range(LRU_BLOCKS)], axis=1)
    rg = _sigmoid(ra + ba_ref[...])
    ig = _sigmoid(ri + bi_ref[...])
    log_a = -LRU_C * rg * _softplus(-lam_ref[...])
    a = jnp.where(valid_c, jnp.exp(log_a), 1.0)
    th = jnp.tanh(log_a)
    bv = jnp.where(valid_c, jnp.sqrt(-2.0 * th / (1.0 - th)) * (ig * xl), 0.0)
    i_in = r % rps if rps != R else r
    step = 1
    while step < rps:
        a_sh = pltpu.roll(a, step, 0)
        b_sh = pltpu.roll(bv, step, 0)
        m = i_in >= step
        bv = jnp.where(m, a * b_sh + bv, bv)
        a = jnp.where(m, a * a_sh, a)
        step *= 2
    h0 = _seq_select(seq_c, nseq, lambda s: hs_ref[s])
    h_seq = a * h0 + bv
    for s in range(nseq):
        lr = s * rps + rps - 1 if nseq > 1 else R - 1
        hs_ref[s] = h_seq[lr:lr + 1, :]
    o_ref[...] = (h_seq * _gelu_tanh(yl_ref[...].astype(F32))).astype(o_ref.dtype)


def _lru(pab, prev, h0, conv_w, conv_b, wa, ba, wi, bi, lam, *, B, R, rps, nseq, nc, rb0, l_valid,
         xcol, ycol):
    T = pab.shape[0]
    W = lam.shape[1]
    bw = W // LRU_BLOCKS
    tile_rows = prev.shape[1]
    nb = B // nseq

    def rowmap(col):
        return lambda b, j: (rb0 + b * nc + j, col)

    def const(b, j):
        return (0, 0)

    def const3(b, j):
        return (0, 0, 0)

    body = functools.partial(_lru_body, R=R, rps=rps, nseq=nseq, l_valid=l_valid, tile_rows=tile_rows)
    st = pl.BlockSpec((nseq, 1, W), lambda b, j: (b, 0, 0))
    vec = pl.BlockSpec((1, W), const)
    return pl.pallas_call(
        body, grid=(nb, nc),
        in_specs=[pl.BlockSpec((R, W), rowmap(xcol)), pl.BlockSpec((R, W), rowmap(ycol)),
                  pl.BlockSpec((nseq, tile_rows, W), lambda b, j: (b, 0, 0)), st,
                  pl.BlockSpec((CONV_W, W), const), vec,
                  pl.BlockSpec((LRU_BLOCKS, bw, bw), const3), vec,
                  pl.BlockSpec((LRU_BLOCKS, bw, bw), const3), vec, vec],
        out_specs=[pl.BlockSpec((R, W), lambda b, j: (b * nc + j, 0)), st],
        out_shape=[jax.ShapeDtypeStruct((nb * nc * R, W), BF16), jax.ShapeDtypeStruct(h0.shape, F32)],
        scratch_shapes=[pltpu.VMEM((nseq * tile_rows, W), F32)],
        compiler_params=_cparams(("parallel", "arbitrary")), name="lru",
    )(pab, pab, prev, h0, conv_w, conv_b, wa, ba, wi, bi, lam)


def _outproj_body(hmp_ref, hsp_ref, hlp_ref, hms_ref, hss_ref, hls_ref, x_ref, wm_ref, ws_ref, wl_ref,
                  gs_ref, g_ref, b_ref, wr_ref, br_ref, x1_ref, x1b_ref, idx_ref, gate_ref, *, alpha, npb):
    is_p = pl.program_id(0) < npb
    hm = jnp.where(is_p, hmp_ref[...], hms_ref[...])
    hl = jnp.where(is_p, hlp_ref[...], hls_ref[...])
    ysf = jnp.where(is_p, hsp_ref[...], hss_ref[...]).astype(F32)
    ysn = ysf * lax.rsqrt(jnp.mean(ysf * ysf, axis=1, keepdims=True) + RMS_EPS) * gs_ref[...]
    y = _dot(hm, wm_ref[...]) + _dot(ysn.astype(BF16), ws_ref[...]) + _dot(hl, wl_ref[...])
    x1 = _layer_norm(alpha * x_ref[...] + y, g_ref[...], b_ref[...])
    x1b = x1.astype(BF16)
    x1_ref[...] = x1
    x1b_ref[...] = x1b
    logits = _dot(x1b, wr_ref[...]) + br_ref[...]
    lane = lax.broadcasted_iota(jnp.int32, logits.shape, 1)
    idx_out = jnp.zeros(logits.shape, jnp.int32)
    val_out = jnp.zeros(logits.shape, F32)
    top = None
    for k in range(TOP_K):
        mx = jnp.max(logits, axis=1, keepdims=True)
        sel = jnp.min(jnp.where(logits == mx, lane, LANES), axis=1, keepdims=True)
        if top is None:
            top = mx
        idx_out = jnp.where(lane == k, sel, idx_out)
        val_out = jnp.where(lane == k, jnp.exp(mx - top), val_out)
        logits = jnp.where(lane == sel, -jnp.inf, logits)
    gate_ref[...] = val_out / jnp.sum(val_out, axis=1, keepdims=True)
    idx_ref[...] = idx_out


def _outproj(mix_p, mix_s, x, wm, ws, wl, g_ssd, ln_g, ln_b, wr, br, alpha):
    T, D = x.shape
    tm = OUT_TILE
    npb = mix_p[0].shape[0] // tm
    assert npb * tm == mix_p[0].shape[0] and mix_s[0].shape[0] % tm == 0

    def rows(w):
        return pl.BlockSpec((tm, w), lambda i: (i, 0))

    def rows_p(a):
        return pl.BlockSpec((tm, a.shape[1]), lambda i: (jnp.minimum(i, npb - 1), 0))

    def rows_s(a):
        return pl.BlockSpec((tm, a.shape[1]), lambda i: (jnp.maximum(i - npb, 0), 0))

    def whole(a):
        return pl.BlockSpec(a.shape, lambda i: (0, 0), pipeline_mode=pl.Buffered(1))

    return pl.pallas_call(
        functools.partial(_outproj_body, alpha=alpha, npb=npb), grid=(T // tm,),
        in_specs=[rows_p(mix_p[0]), rows_p(mix_p[1]), rows_p(mix_p[2]),
                  rows_s(mix_s[0]), rows_s(mix_s[1]), rows_s(mix_s[2]), rows(D),
                  whole(wm), whole(ws), whole(wl), whole(g_ssd), whole(ln_g), whole(ln_b),
                  whole(wr), whole(br)],
        out_specs=[rows(D), rows(D), rows(LANES), rows(LANES)],
        out_shape=[jax.ShapeDtypeStruct((T, D), F32), jax.ShapeDtypeStruct((T, D), BF16),
                   jax.ShapeDtypeStruct((T, LANES), jnp.int32), jax.ShapeDtypeStruct((T, LANES), F32)],
        compiler_params=_cparams(("parallel",)), name="outproj_ln_router",
    )(*mix_p, *mix_s, x, wm, ws, wl, g_ssd, ln_g, ln_b, wr, br)


def _expert_body(ie_ref, ir0_ref, ins_ref, inv_ref, na_ref, tok_ref, dst_ref,
                 x_hbm, wg_ref, bg_ref, wu_ref, bu_ref, wd_ref, bd_ref, yg_in, yg_hbm,
                 xbuf, acc, sem_in, sem_out, *, n_items, nf):
    del yg_in
    i = pl.program_id(0)
    j = pl.program_id(1)
    slot = i % 2
    half = wg_ref.shape[2] // 2

    def row_in(it_slot, tok, r):
        return pltpu.make_async_copy(x_hbm.at[pl.ds(tok, 1)], xbuf.at[it_slot, pl.ds(r, 1)], sem_in.at[it_slot])

    def row_out(r, dst):
        return pltpu.make_async_copy(acc.at[pl.ds(r, 1)], yg_hbm.at[pl.ds(dst, 1)], sem_out.at[0])

    def start_gather(it, it_slot):
        r0 = ir0_ref[it]

        def go(g, c):
            for u in range(DMA_UNROLL):
                r = g * DMA_UNROLL + u
                row_in(it_slot, tok_ref[r0 + r], r).start()
            return c
        lax.fori_loop(0, ins_ref[it] * (MOE_SUB // DMA_UNROLL), go, 0)

    def wait_gather(it, it_slot):
        def go(g, c):
            for u in range(DMA_UNROLL):
                row_in(it_slot, 0, 0).wait()
            return c
        lax.fori_loop(0, ins_ref[it] * (MOE_SUB // DMA_UNROLL), go, 0)

    @pl.when(j == 0)
    def _():
        @pl.when(i == 0)
        def _():
            start_gather(0, 0)

        for static_slot in range(2):
            @pl.when(slot == static_slot)
            def _():
                wait_gather(i, static_slot)

                @pl.when(i + 1 < n_items)
                def _():
                    start_gather(i + 1, 1 - static_slot)

    wg = wg_ref[0, 0].astype(BF16)
    wu = wu_ref[0, 0].astype(BF16)
    wd = wd_ref[0, 0].astype(BF16)
    bg = bg_ref[0, 0]
    bu = bu_ref[0, 0]
    r0 = ir0_ref[i]
    n_valid = inv_ref[i]

    def sub_block(s, c):
        rows = pl.ds(pl.multiple_of(s * MOE_SUB, MOE_SUB), MOE_SUB)
        w = xbuf[slot, rows, :]
        lo = lax.bitcast_convert_type(w << 16, F32).astype(BF16)
        hi = lax.bitcast_convert_type(w & jnp.uint32(0xFFFF0000), F32).astype(BF16)
        gate = jnp.minimum(_dot(lo, wg[:half]) + _dot(hi, wg[half:]) + bg, SWIGLU_LIMIT)
        up = jnp.clip(_dot(lo, wu[:half]) + _dot(hi, wu[half:]) + bu, -SWIGLU_LIMIT, SWIGLU_LIMIT)
        act = (up + 1.0) * gate * _sigmoid(SWIGLU_ALPHA * gate)
        contrib = _dot(act.astype(BF16), wd)

        @pl.when(j == 0)
        def _():
            acc[rows, :] = contrib + bd_ref[0, 0]

        @pl.when(j != 0)
        def _():
            acc[rows, :] += contrib

        @pl.when(j == nf - 1)
        def _():
            cnt = jnp.clip(n_valid - s * MOE_SUB, 0, MOE_SUB)

            def go8(g, cc):
                for u in range(DMA_UNROLL):
                    r = s * MOE_SUB + g * DMA_UNROLL + u
                    row_out(r, dst_ref[r0 + r]).start()
                return cc
            lax.fori_loop(0, cnt // DMA_UNROLL, go8, 0)

            def go1(q, cc):
                r = s * MOE_SUB + q
                row_out(r, dst_ref[r0 + r]).start()
                return cc
            lax.fori_loop(cnt // DMA_UNROLL * DMA_UNROLL, cnt, go1, 0)
        return c

    lax.fori_loop(0, ins_ref[i], sub_block, 0)

    @pl.when(j == nf - 1)
    def _():
        def go8(g, c):
            for u in range(DMA_UNROLL):
                row_out(0, 0).wait()
            return c
        lax.fori_loop(0, n_valid // DMA_UNROLL, go8, 0)

        def go1(q, c):
            row_out(0, 0).wait()
            return c
        lax.fori_loop(n_valid // DMA_UNROLL * DMA_UNROLL, n_valid, go1, 0)


def _experts(xw, items, tok, dst, wg, bg, wu, bu, wd, bd, yg, l):
    Fh = wg.shape[3]
    D = wg.shape[2]
    item_e, item_r0, item_ns, item_nv, n_act = items
    n_items = item_e.shape[0]
    nf = Fh // MOE_TF
    cap = MOE_NSUB * MOE_SUB

    def jeff(i, j, na):
        return jnp.where(i < na[0], j, nf - 1)

    def wmap(i, j, ie, ir0, ins, inv, na, tk, ds):
        return (l, ie[i], 0, jeff(i, j, na))

    def wdmap(i, j, ie, ir0, ins, inv, na, tk, ds):
        return (l, ie[i], jeff(i, j, na), 0)

    def bdmap(i, j, ie, ir0, ins, inv, na, tk, ds):
        return (l, ie[i], 0, 0)

    grid_spec = pltpu.PrefetchScalarGridSpec(
        num_scalar_prefetch=7, grid=(n_items, nf),
        in_specs=[pl.BlockSpec(memory_space=pl.ANY),
                  pl.BlockSpec((1, 1, D, MOE_TF), wmap), pl.BlockSpec((1, 1, 1, MOE_TF), wmap),
                  pl.BlockSpec((1, 1, D, MOE_TF), wmap), pl.BlockSpec((1, 1, 1, MOE_TF), wmap),
                  pl.BlockSpec((1, 1, MOE_TF, D), wdmap), pl.BlockSpec((1, 1, 1, D), bdmap),
                  pl.BlockSpec(memory_space=pl.ANY)],
        out_specs=pl.BlockSpec(memory_space=pl.ANY),
        scratch_shapes=[pltpu.VMEM((2, cap, D // 2), jnp.uint32), pltpu.VMEM((cap, D), F32),
                        pltpu.SemaphoreType.DMA((2,)), pltpu.SemaphoreType.DMA((1,))])
    return pl.pallas_call(
        functools.partial(_expert_body, n_items=n_items, nf=nf), grid_spec=grid_spec,
        out_shape=jax.ShapeDtypeStruct(yg.shape, F32), input_output_aliases={14: 0},
        compiler_params=_cparams(("arbitrary", "arbitrary")), name="moe_experts",
    )(item_e, item_r0, item_ns, item_nv, n_act, tok, dst, xw, wg, bg, wu, bu, wd, bd, yg)


def _combine_body(yg_ref, gate_ref, x_ref, g_ref, b_ref, o_ref, ob_ref, *, alpha):
    gates = gate_ref[...]
    f = gates[:, 0:1] * yg_ref[0]
    for k in range(1, TOP_K):
        f = f + gates[:, k:k + 1] * yg_ref[k]
    x2 = _layer_norm(alpha * x_ref[...] + f, g_ref[...], b_ref[...])
    o_ref[...] = x2
    ob_ref[...] = x2.astype(BF16)


def _combine_ln(yg, gates, x, ln_g, ln_b, alpha):
    T, D = x.shape
    tm = OUT_TILE
    rows = pl.BlockSpec((tm, D), lambda i: (i, 0))
    vec = pl.BlockSpec((1, D), lambda i: (0, 0))
    return pl.pallas_call(
        functools.partial(_combine_body, alpha=alpha), grid=(T // tm,),
        in_specs=[pl.BlockSpec((TOP_K, tm, D), lambda i: (0, i, 0)),
                  pl.BlockSpec((tm, LANES), lambda i: (i, 0)), rows, vec, vec],
        out_specs=[rows, rows],
        out_shape=[jax.ShapeDtypeStruct((T, D), F32), jax.ShapeDtypeStruct((T, D), BF16)],
        compiler_params=_cparams(("parallel",)), name="moe_combine_ln",
    )(yg, gates, x, ln_g, ln_b)


def _moe_dispatch(idx, valid_tok, n_experts, n_rows, n_items):
    T = idx.shape[0]
    cap = MOE_NSUB * MOE_SUB
    flat_e = jnp.where(valid_tok[:, None], idx[:, :TOP_K], n_experts).reshape(-1)
    order = jnp.argsort(flat_e)
    counts = jnp.bincount(flat_e, length=n_experts + 1)[:n_experts]
    nsub = (counts + MOE_SUB - 1) // MOE_SUB
    pad_end = jnp.cumsum(nsub * MOE_SUB)
    pad_start = pad_end - nsub * MOE_SUB
    start = jnp.cumsum(counts) - counts
    nit = (nsub + MOE_NSUB - 1) // MOE_NSUB
    it_end = jnp.cumsum(nit)
    it_start = it_end - nit
    n_act = it_end[-1]
    t = jnp.arange(n_items)
    e_t = jnp.minimum(jnp.searchsorted(it_end, t, side="right"), n_experts - 1)
    k = t - it_start[e_t]
    active = t < n_act
    e_last = e_t[jnp.maximum(n_act - 1, 0)]
    item_e = jnp.where(active, e_t, e_last)
    item_r0 = jnp.where(active, pad_start[e_t] + k * cap, 0)
    item_ns = jnp.where(active, jnp.minimum(nsub[e_t] - k * MOE_NSUB, MOE_NSUB), 0)
    item_nv = jnp.where(active, jnp.clip(counts[e_t] - k * cap, 0, cap), 0)
    items = tuple(a.astype(jnp.int32) for a in (item_e, item_r0, item_ns, item_nv, n_act.reshape(1)))
    p = jnp.arange(n_rows)
    row_e = jnp.minimum(jnp.searchsorted(pad_end, p, side="right"), n_experts - 1)
    off = p - pad_start[row_e]
    valid = (off < counts[row_e]) & (p < pad_end[-1])
    a = order[jnp.where(valid, start[row_e] + off, 0)]
    tok = jnp.where(valid, a // TOP_K, 0).astype(jnp.int32)
    dst = jnp.where(valid, (a % TOP_K) * T + a // TOP_K, -1).astype(jnp.int32)
    return items, tok, dst


def kernel(x_prompt, x_sample, state_mlstm_C, state_mlstm_n, state_mlstm_m, state_ssd_h, state_ssd_conv,
           state_lru_h, state_lru_conv, meta, ln_in_g, ln_in_b, w_in, b_if, conv_ssd_w, conv_ssd_b, dt_bias,
           a_log, d_skip, g_ssd, g_mlstm, conv_lru_w, conv_lru_b, w_rg_a, b_rg_a, w_rg_i, b_rg_i, lam, w_out,
           ln1_g, ln1_b, w_router, b_router, w_gate, b_gate, w_up, b_up, w_down, b_down, ln2_g, ln2_b):
    Bp, S, D = x_prompt.shape
    Bs, Ls, _ = x_sample.shape
    depth = w_in.shape[0]
    n_meta = meta.shape[0]
    nh = MLSTM_HEADS
    MW = g_mlstm.shape[1]
    SW = g_ssd.shape[1]
    LW = lam.shape[1]
    CW = conv_ssd_w.shape[2]
    n_ssd_heads = dt_bias.shape[1]
    E = w_router.shape[2]
    alpha = (2 * depth) ** 0.25
    Lp = n_meta + S
    LP = -(-Lp // CHUNK) * CHUNK
    LS = SAMPLE_LS
    Tp, Ts = Bp * LP, Bs * LS
    T = Tp + Ts
    ncp = LP // CHUNK
    RS = SAMPLE_SEQS * LS
    assert CONV_W - 1 <= Ls <= LS and T % ROW_TILE == 0 and T % OUT_TILE == 0 and Bs % SAMPLE_SEQS == 0 and Tp % RS == 0

    xp = jnp.concatenate([jnp.broadcast_to(meta.astype(F32)[None], (Bp, n_meta, D)), x_prompt,
                          jnp.zeros((Bp, LP - Lp, D), F32)], axis=1).reshape(Tp, D)
    xs = jnp.concatenate([x_sample, jnp.zeros((Bs, LS - Ls, D), F32)], axis=1).reshape(Ts, D)
    x, xb = _ln_in(jnp.concatenate([xp, xs], axis=0), ln_in_g, ln_in_b)
    pos = np.concatenate([np.tile(np.arange(LP) < Lp, Bp), np.tile(np.arange(LS) < Ls, Bs)])
    valid_tok = jnp.asarray(pos)

    sizes = [MW, MW, MW, MW, nh, nh, SW, CW, n_ssd_heads, LW, LW]
    o = [0] + [int(v) for v in np.cumsum(sizes)]
    w_ab = jnp.concatenate([w_in[:, :, o[0]:o[4]], w_in[:, :, o[6]:o[7]], w_in[:, :, o[9]:o[11]]],
                           axis=2).astype(BF16)
    w_c = w_in[:, :, o[7]:o[8]].astype(BF16)
    n_gate = 2 * nh + n_ssd_heads
    w_d = jnp.concatenate([w_in[:, :, o[4]:o[6]], w_in[:, :, o[8]:o[9]],
                           jnp.zeros((depth, D, LANES - n_gate), F32)], axis=2).astype(BF16)
    zcol = (4 * MW) // SW
    xcol = (4 * MW + SW) // LW
    ycol = xcol + 1
    assert zcol * SW == 4 * MW and xcol * LW == 4 * MW + SW

    def lane_pad(v, fill=0.0, front=0):
        return jnp.concatenate([jnp.full((front,), fill, F32), v.astype(F32),
                                jnp.full((LANES - front - v.shape[0],), fill, F32)])

    w_out_b = w_out.astype(BF16)
    wr_b = jnp.concatenate([w_router, jnp.zeros((depth, D, LANES - E), F32)], axis=2).astype(BF16)
    n_assign = (Bp * Lp + Bs * Ls) * TOP_K
    n_rows = -(-(n_assign + E * (MOE_SUB - 1)) // MOE_SUB) * MOE_SUB
    n_items = E + -(-(n_rows // MOE_SUB) // MOE_NSUB)
    yg = jnp.zeros((TOP_K * T, D), F32)

    zeros_p = dict(
        c=jnp.zeros((1, Bp, nh, MW // nh, MW // nh), F32), n=jnp.zeros((Bp, nh, MW // nh), F32),
        m=jnp.zeros((Bp, 1, nh), F32), h=jnp.zeros((1, Bp, n_ssd_heads, SSD_HEADDIM, SSD_STATE), F32),
        cs=jnp.zeros((Bp, 16, CW), F32), l=jnp.zeros((Bp, 1, LW), F32), cl=jnp.zeros((Bp, 16, LW), F32))
    pgrp = dict(B=Bp, R=CHUNK, rps=CHUNK, nseq=1, nc=ncp, rb0=0, l_valid=Lp)
    sgrp = dict(B=Bs, R=RS, rps=LS, nseq=SAMPLE_SEQS, nc=1, rb0=Tp // RS, l_valid=Ls)

    outs = [[] for _ in range(14)]
    bg4, bu4, bd4 = (b.reshape(depth, E, 1, b.shape[2]) for b in (b_gate, b_up, b_down))
    s_c_all = jnp.zeros(state_mlstm_C.shape, F32)
    s_h_all = jnp.zeros(state_ssd_h.shape, F32)
    for l in range(depth):
        pab = _matmul(xb, w_ab, l, BF16, 1024, "inproj_ab")
        pc = _matmul(xb, w_c, l, BF16, CW // 2, "inproj_c")
        gates = _matmul(xb, w_d, l, F32, LANES, "inproj_gates")
        gt_p = gates[:Tp].reshape(Bp * ncp, CHUNK, LANES).transpose(0, 2, 1)
        gt_s = gates[Tp:].reshape(Bs // SAMPLE_SEQS, RS, LANES).transpose(0, 2, 1)
        gt_s = jnp.concatenate([gt_s, jnp.zeros((Bs // SAMPLE_SEQS, LANES, CHUNK - RS), F32)], axis=2)

        bif_r = lane_pad(b_if[l]).reshape(1, LANES)
        bif_c = bif_r.reshape(LANES, 1)
        dtb_r = lane_pad(dt_bias[l], front=2 * nh).reshape(1, LANES)
        al_r = lane_pad(a_log[l], front=2 * nh).reshape(1, LANES)
        dsk = jnp.repeat(d_skip[l].astype(F32), SSD_HEADDIM).reshape(1, SW)
        gm = g_mlstm[l].reshape(1, MW).astype(F32)
        cw_s, cb_s = conv_ssd_w[l].astype(F32), conv_ssd_b[l].reshape(1, CW).astype(F32)
        cw_l, cb_l = conv_lru_w[l].astype(F32), conv_lru_b[l].reshape(1, LW).astype(F32)
        wa, wi = w_rg_a[l].astype(BF16), w_rg_i[l].astype(BF16)
        ba, bi = b_rg_a[l].reshape(1, LW).astype(F32), b_rg_i[l].reshape(1, LW).astype(F32)
        lam_l = lam[l].reshape(1, LW).astype(F32)

        def pad_conv(buf):
            return jnp.concatenate([jnp.zeros((buf.shape[0], SUBLANES - (CONV_W - 1), buf.shape[2]), F32),
                                    buf.astype(F32)], axis=1)

        st_s = dict(c=state_mlstm_C.astype(F32), n=state_mlstm_n[l].astype(F32),
                    m=state_mlstm_m[l].astype(F32).reshape(Bs, 1, nh), h=state_ssd_h.astype(F32),
                    cs=pad_conv(state_ssd_conv[l]), l=state_lru_h[l].astype(F32).reshape(Bs, 1, LW),
                    cl=pad_conv(state_lru_conv[l]))

        mixes, new_states = [], []
        for gi, (grp, st, gt) in enumerate(((pgrp, zeros_p, gt_p), (sgrp, st_s, gt_s))):
            lay_c = (0, 0, 1, None) if gi == 0 else (l, l, depth, s_c_all)
            lay_h = (0, 0, 1, None) if gi == 0 else (l, l, depth, s_h_all)
            hm_g, c_n, n_n, m_n = _mlstm(pab, gates, gt, bif_r, bif_c, gm, st["n"], st["m"], st["c"], lay_c, **grp)
            hs_g, h_n = _ssd(pc, pab, gates, gt, st["cs"], cw_s, cb_s, dtb_r, dtb_r.reshape(LANES, 1),
                             al_r, al_r.reshape(LANES, 1), dsk, st["h"], lay_h, zcol=zcol, **grp)
            if gi == 1:
                s_c_all, s_h_all = c_n, h_n
            hl_g, l_n = _lru(pab, st["cl"], st["l"], cw_l, cb_l, wa, ba, wi, bi, lam_l, xcol=xcol, ycol=ycol,
                             **grp)
            new_states.append((c_n, n_n, m_n, h_n, l_n))
            mixes.append((hm_g, hs_g, hl_g))

        x1, x1b, idx, gate_w = _outproj(
            mixes[0], mixes[1], x, w_out_b[l, :MW], w_out_b[l, MW:MW + SW], w_out_b[l, MW + SW:],
            g_ssd[l].reshape(1, SW).astype(F32), ln1_g[l].reshape(1, D), ln1_b[l].reshape(1, D),
            wr_b[l], lane_pad(b_router[l], fill=NEG).reshape(1, LANES), alpha)

        items, row_tok, row_dst = _moe_dispatch(idx, valid_tok, E, n_rows, n_items)
        bits = lax.bitcast_convert_type(x1b, jnp.uint16).astype(jnp.uint32)
        xw = bits[:, :D // 2] | (bits[:, D // 2:] << 16)
        yg = _experts(xw, items, row_tok, row_dst, w_gate, bg4, w_up, bu4, w_down, bd4, yg, l)
        x, xb = _combine_ln(yg.reshape(TOP_K, T, D), gate_w, x1, ln2_g[l].reshape(1, D), ln2_b[l].reshape(1, D),
                            alpha)

        xbc_p = pc[:Tp].reshape(Bp, LP, CW)[:, Lp - (CONV_W - 1):Lp].astype(F32)
        xbc_s = pc[Tp:].reshape(Bs, LS, CW)[:, Ls - (CONV_W - 1):Ls].astype(F32)
        xl_p = pab[:Tp, xcol * LW:(xcol + 1) * LW].reshape(Bp, LP, LW)[:, Lp - (CONV_W - 1):Lp].astype(F32)
        xl_s = pab[Tp:, xcol * LW:(xcol + 1) * LW].reshape(Bs, LS, LW)[:, Ls - (CONV_W - 1):Ls].astype(F32)
        for gi, (bsz, conv_s, conv_l) in enumerate(((Bp, xbc_p, xl_p), (Bs, xbc_s, xl_s))):
            c_n, n_n, m_n, h_n, l_n = new_states[gi]
            vals = (c_n[0], n_n, m_n.reshape(bsz, nh), h_n[0], conv_s, l_n.reshape(bsz, LW), conv_l)
            for k, v in enumerate(vals):
                if gi == 0 or k not in (0, 3):
                    outs[gi * 7 + k].append(v)

    y_prompt = x[:Tp].reshape(Bp, LP, D)[:, n_meta:Lp]
    y_sample = x[Tp:].reshape(Bs, LS, D)[:, :Ls]
    state_dtypes = [state_mlstm_C.dtype, state_mlstm_n.dtype, state_mlstm_m.dtype, state_ssd_h.dtype,
                    state_ssd_conv.dtype, state_lru_h.dtype, state_lru_conv.dtype]
    stacked = [jnp.stack(v).astype(state_dtypes[k % 7]) if v else None for k, v in enumerate(outs)]
    stacked[7] = s_c_all.astype(state_dtypes[0])
    stacked[7 + 3] = s_h_all.astype(state_dtypes[3])
    return (y_prompt, y_sample, *stacked)
```

```python
import functools
import math

import numpy as np
import jax
import jax.numpy as jnp
from jax import lax
from jax.experimental import pallas as pl
from jax.experimental.pallas import tpu as pltpu

F32 = jnp.float32
BF16 = jnp.bfloat16

LANES = 128
SUBLANES = 8
VMEM_LIMIT = 56 * 1024 * 1024

CONV_W = 4
MLSTM_HEADS = 4
SSD_HEADDIM = 64
SSD_STATE = 128
SSD_GROUPS = 2
LRU_BLOCKS = 8
LRU_C = 8.0
TOP_K = 4
SWIGLU_LIMIT = 7.0
SWIGLU_ALPHA = 1.702
LN_EPS = 1e-5
RMS_EPS = 1e-6
NEG = -1e30

CHUNK = 128
SAMPLE_LS = 8
SAMPLE_SEQS = 4
ROW_TILE = 512
MOE_SUB = 256
MOE_NSUB = 5
MOE_TF = 256
MOE_TN = 512
DMA_UNROLL = 8
OUT_TILE = 256


def _cparams(sem):
    return pltpu.CompilerParams(dimension_semantics=sem, vmem_limit_bytes=VMEM_LIMIT)


def _softplus(x):
    return jnp.maximum(x, 0.0) + jnp.log1p(jnp.exp(-jnp.abs(x)))


def _log_sigmoid(x):
    return -_softplus(-x)


def _sigmoid(x):
    return 1.0 / (1.0 + jnp.exp(-x))


def _gelu_tanh(x):
    return 0.5 * x * (1.0 + jnp.tanh(math.sqrt(2.0 / math.pi) * (x + 0.044715 * (x * x * x))))


def _layer_norm(x, g, b):
    mu = jnp.mean(x, axis=-1, keepdims=True)
    xc = x - mu
    var = jnp.mean(xc * xc, axis=-1, keepdims=True)
    return xc * lax.rsqrt(var + LN_EPS) * g + b


def _dot(a, b, precision=None):
    return jnp.dot(a, b, preferred_element_type=F32, precision=precision)


def _split_bf16(x):
    hi = x.astype(BF16)
    return hi, (x - hi.astype(F32)).astype(BF16)


def _dot_sel(x, sel):
    hi, lo = _split_bf16(x)
    return _dot(hi, sel) + _dot(lo, sel)


def _sel_dot(sel, x):
    hi, lo = _split_bf16(x)
    return _dot(sel, hi) + _dot(sel, lo)


def _dot_nt(a, b):
    return lax.dot_general(a, b, (((1,), (1,)), ((), ())), preferred_element_type=F32)


def _dot_tn(a, b):
    return lax.dot_general(a, b, (((0,), (0,)), ((), ())), preferred_element_type=F32)


def _ln_in_body(x_ref, g_ref, b_ref, o_ref, ob_ref):
    y = _layer_norm(x_ref[...], g_ref[...], b_ref[...])
    o_ref[...] = y
    ob_ref[...] = y.astype(BF16)


def _ln_in(x, g, b):
    T, D = x.shape
    row = pl.BlockSpec((ROW_TILE, D), lambda i: (i, 0))
    vec = pl.BlockSpec((1, D), lambda i: (0, 0))
    return pl.pallas_call(
        _ln_in_body, grid=(T // ROW_TILE,), in_specs=[row, vec, vec], out_specs=[row, row],
        out_shape=[jax.ShapeDtypeStruct((T, D), F32), jax.ShapeDtypeStruct((T, D), BF16)],
        compiler_params=_cparams(("parallel",)), name="ln_in")(x, g.reshape(1, D), b.reshape(1, D))


def _mm_body(x_ref, w_ref, o_ref):
    o_ref[...] = _dot(x_ref[...], w_ref[0]).astype(o_ref.dtype)


def _matmul(x, w, l, out_dtype, tn, name):
    T, K = x.shape
    N = w.shape[2]
    return pl.pallas_call(
        _mm_body, grid=(N // tn, T // ROW_TILE),
        in_specs=[pl.BlockSpec((ROW_TILE, K), lambda j, i: (i, 0)),
                  pl.BlockSpec((1, K, tn), lambda j, i: (l, 0, j))],
        out_specs=pl.BlockSpec((ROW_TILE, tn), lambda j, i: (i, j)),
        out_shape=jax.ShapeDtypeStruct((T, N), out_dtype),
        compiler_params=_cparams(("parallel", "parallel")), name=name)(x, w)


def _row_ids(R, rps, chunk_rows):
    r = lax.broadcasted_iota(jnp.int32, (R, 1), 0)
    t = lax.broadcasted_iota(jnp.int32, (1, CHUNK), 1)
    if rps == R:
        j = pl.program_id(1)
        return r, t, jnp.zeros_like(r), jnp.zeros_like(t), r + j * chunk_rows, t + j * chunk_rows
    return r, t, r // rps, t // rps, r % rps, t % rps


def _conv_shift(u_bf, prev_f32, R, rps, tile_rows):
    P = prev_f32.shape[0]
    r = lax.broadcasted_iota(jnp.int32, (R, 1), 0)
    ku = lax.broadcasted_iota(jnp.int32, (1, R), 1)
    kp = lax.broadcasted_iota(jnp.int32, (1, P), 1)
    i = r % rps
    seq = r // rps
    taps = [u_bf.astype(F32)]
    for j in range(1, CONV_W):
        su = ((i >= j) & (ku == r - j)).astype(BF16)
        sp = ((i < j) & (kp == seq * tile_rows + (tile_rows - j) + i)).astype(BF16)
        taps.append(_dot(su, u_bf) + _sel_dot(sp, prev_f32))
    return taps


def _seq_select(seq_c, nseq, fn):
    if nseq == 1:
        return fn(0)
    out = None
    for s in range(nseq):
        term = jnp.where(seq_c == s, fn(s), 0.0)
        out = term if out is None else out + term
    return out


def _pad_rows(x, rows):
    if x.shape[0] == rows:
        return x
    return jnp.concatenate([x, jnp.zeros((rows - x.shape[0],) + x.shape[1:], x.dtype)], axis=0)


def _mlstm_body(q_ref, k_ref, v_ref, o_ref, g_ref, gt_ref, bifr_ref, bifc_ref, gm_ref,
                n0_ref, m0_ref, c0_ref, h_ref, c_ref, n_ref, m_ref, *, R, rps, nseq, l_valid):
    nh = MLSTM_HEADS
    dh = q_ref.shape[1] // nh
    j = pl.program_id(1)

    @pl.when(j == 0)
    def _():
        c_ref[...] = c0_ref[...]
        n_ref[...] = n0_ref[...]
        m_ref[...] = m0_ref[...]

    r, t, seq_c, seq_r, pos_c, pos_r = _row_ids(R, rps, R)
    valid_c = pos_c < l_valid
    valid_r = (t < R) & (pos_r < l_valid)
    same = (seq_c == seq_r) & (t < R)
    causal = same & (t <= r)

    pre_c = g_ref[...] + bifr_ref[...]
    pre_r = gt_ref[0] + bifc_ref[...]
    f_c = jnp.where(valid_c, _log_sigmoid(pre_c), 0.0)
    f_r = jnp.where(valid_r, _log_sigmoid(pre_r), 0.0)
    i_c = jnp.where(valid_c, pre_c, NEG)
    i_r = jnp.where(valid_r, pre_r, NEG)
    rq = lax.broadcasted_iota(jnp.int32, (1, R), 1)
    lower = (((r // rps) == (rq // rps)) & (rq <= r)).astype(BF16) if rps != R else (rq <= r).astype(BF16)
    b_c = _sel_dot(lower, f_c)
    tr = lax.broadcasted_iota(jnp.int32, (CHUNK, 1), 0)
    upper = ((tr <= t) & (tr < R) & ((tr // rps) == (t // rps))).astype(BF16)
    b_r = _dot_sel(f_r, upper)

    lane_h = lax.broadcasted_iota(jnp.int32, (1, nh), 1)
    m_old = [m_ref[s] for s in range(nseq)]
    m_new_rows = [jnp.zeros((1, nh), F32) for _ in range(nseq)]

    for h in range(nh):
        sl = slice(h * dh, (h + 1) * dh)
        q = q_ref[:, sl]
        ks = k_ref[:, sl] * jnp.asarray(dh ** -0.5, BF16)
        v = v_ref[:, sl]
        bc = b_c[:, nh + h:nh + h + 1]
        br = b_r[nh + h:nh + h + 1, :]
        ic = i_c[:, h:h + 1]
        ir = i_r[h:h + 1, :]
        log_d = jnp.where(causal, bc - br + ir, NEG)
        mm = _seq_select(seq_c, nseq, lambda s: m_old[s][:, h:h + 1])
        log_p = bc + mm
        m_t = jnp.maximum(log_p, jnp.max(log_d, axis=1, keepdims=True))
        d = jnp.exp(log_d - m_t)
        wp = jnp.exp(log_p - m_t)
        s_mat = _dot_nt(q, _pad_rows(ks, CHUNK)) * d
        qf = q.astype(F32)
        num_state = _seq_select(seq_c, nseq, lambda s: _dot(q, c_ref[0, s, h].astype(BF16)))
        den_state = _seq_select(
            seq_c, nseq, lambda s: jnp.sum(qf * n_ref[s, h:h + 1, :], axis=1, keepdims=True))
        num = _dot(s_mat.astype(BF16), _pad_rows(v, CHUNK)) + wp * num_state
        den = jnp.sum(s_mat, axis=1, keepdims=True) + wp * den_state
        hh = num / jnp.maximum(jnp.abs(den), jnp.exp(-m_t))
        hn = hh * lax.rsqrt(jnp.mean(hh * hh, axis=1, keepdims=True) + RMS_EPS) * gm_ref[:, sl]
        h_ref[:, sl] = (hn * _sigmoid(o_ref[:, sl].astype(F32))).astype(h_ref.dtype)

        last = [s * rps + rps - 1 if nseq > 1 else R - 1 for s in range(nseq)]
        b_end = _seq_select(seq_c, nseq, lambda s: bc[last[s]:last[s] + 1, :])
        m_end = _seq_select(seq_c, nseq, lambda s: m_t[last[s]:last[s] + 1, :])
        w_s = jnp.exp(b_end - bc + ic - m_end)
        kw = ks.astype(F32) * w_s
        for s in range(nseq):
            be = bc[last[s]:last[s] + 1, :]
            mn = m_t[last[s]:last[s] + 1, :]
            w_prev = jnp.exp(be + m_old[s][:, h:h + 1] - mn)
            kw_s = kw if nseq == 1 else jnp.where(seq_c == s, kw, 0.0)
            c_ref[0, s, h] = w_prev * c_ref[0, s, h] + _dot_tn(kw_s.astype(BF16), v)
            n_ref[s, h:h + 1, :] = w_prev * n_ref[s, h:h + 1, :] + jnp.sum(kw_s, axis=0, keepdims=True)
            m_new_rows[s] = m_new_rows[s] + jnp.where(lane_h == h, mn, 0.0)

    for s in range(nseq):
        m_ref[s] = m_new_rows[s]


def _drop_alias(body, n_in):
    def wrapped(*refs):
        return body(*refs[:n_in], *refs[n_in + 1:])
    return wrapped


def _layered_state_call(body, grid, in_specs, args, out_specs, out_shapes, big_out, state_in, state_blk,
                        layers, name, scratch_shapes=()):
    l_in, l_out, n_out_layers, prev_out = layers
    zeros = (0,) * (len(state_blk) - 1)
    in_specs = list(in_specs) + [pl.BlockSpec((1,) + state_blk, lambda b, j: (l_in, b) + zeros)]
    args = list(args) + [state_in]
    out_specs = list(out_specs)
    out_specs.insert(big_out, pl.BlockSpec((1,) + state_blk, lambda b, j: (l_out, b) + zeros))
    out_shapes = list(out_shapes)
    out_shapes.insert(big_out, jax.ShapeDtypeStruct((n_out_layers,) + state_in.shape[1:], F32))
    aliases = {}
    if prev_out is not None:
        body = _drop_alias(body, len(args))
        aliases = {len(args): big_out}
        in_specs.append(pl.BlockSpec(memory_space=pl.ANY))
        args.append(prev_out)
    return pl.pallas_call(
        body, grid=grid, in_specs=in_specs, out_specs=out_specs, out_shape=out_shapes,
        input_output_aliases=aliases, scratch_shapes=list(scratch_shapes),
        compiler_params=_cparams(("parallel", "arbitrary")), name=name)(*args)


def _mlstm(pab, gates, gates_t, bif_r, bif_c, g_m, n0, m0, c_all, layers, *, B, R, rps, nseq, nc, rb0,
           l_valid):
    nh = MLSTM_HEADS
    W = g_m.shape[1]
    dh = W // nh

    def rowmap(col):
        return lambda b, j: (rb0 + b * nc + j, col)

    def const(b, j):
        return (0, 0)

    nb = B // nseq
    body = functools.partial(_mlstm_body, R=R, rps=rps, nseq=nseq, l_valid=l_valid)
    st3 = pl.BlockSpec((nseq, nh, dh), lambda b, j: (b, 0, 0))
    stm = pl.BlockSpec((nseq, 1, nh), lambda b, j: (b, 0, 0))
    return _layered_state_call(
        body, (nb, nc),
        [pl.BlockSpec((R, W), rowmap(0)), pl.BlockSpec((R, W), rowmap(1)),
         pl.BlockSpec((R, W), rowmap(2)), pl.BlockSpec((R, W), rowmap(3)),
         pl.BlockSpec((R, LANES), rowmap(0)),
         pl.BlockSpec((1, LANES, CHUNK), lambda b, j: (b * nc + j, 0, 0)),
         pl.BlockSpec((1, LANES), const), pl.BlockSpec((LANES, 1), const),
         pl.BlockSpec((1, W), const), st3, stm],
        [pab, pab, pab, pab, gates, gates_t, bif_r, bif_c, g_m, n0, m0],
        [pl.BlockSpec((R, W), lambda b, j: (b * nc + j, 0)), st3, stm],
        [jax.ShapeDtypeStruct((nb * nc * R, W), BF16), jax.ShapeDtypeStruct(n0.shape, F32),
         jax.ShapeDtypeStruct(m0.shape, F32)],
        1, c_all, (nseq, nh, dh, dh), layers, "mlstm")


def _ssd_body(xbc_ref, z_ref, g_ref, gt_ref, prev_ref, cw_ref, cb_ref, dtbr_ref, dtbc_ref,
              alr_ref, alc_ref, dsk_ref, h0_ref, y_ref, hs_ref, prev_s, *, R, rps, nseq, l_valid, tile_rows):
    nheads = h0_ref.shape[2]
    P = SSD_HEADDIM
    N = SSD_STATE
    G = SSD_GROUPS
    hg = nheads // G
    W = nheads * P
    j = pl.program_id(1)

    @pl.when(j == 0)
    def _():
        hs_ref[...] = h0_ref[...]
        prev_s[...] = prev_ref[...].reshape(prev_s.shape)

    r, t, seq_c, seq_r, pos_c, pos_r = _row_ids(R, rps, R)
    valid_c = pos_c < l_valid
    valid_r = (t < R) & (pos_r < l_valid)
    causal = (seq_c == seq_r) & (t < R) & (t <= r)

    u = xbc_ref[...]
    taps = _conv_shift(u, prev_s[...], R, rps, tile_rows)
    if nseq == 1:
        prev_s[...] = u[R - tile_rows:, :].astype(F32)
    acc = cb_ref[...] + taps[0] * cw_ref[CONV_W - 1:CONV_W, :]
    for jj in range(1, CONV_W):
        acc = acc + taps[jj] * cw_ref[CONV_W - 1 - jj:CONV_W - jj, :]
    xc = acc * _sigmoid(acc)
    x = xc[:, :W]
    bm = xc[:, W:W + G * N]
    cm = xc[:, W + G * N:]

    off = 2 * MLSTM_HEADS
    dt_c = jnp.where(valid_c, _softplus(g_ref[...] + dtbr_ref[...]), 0.0)
    dt_r = jnp.where(valid_r, _softplus(gt_ref[0] + dtbc_ref[...]), 0.0)
    adt_c = dt_c * (-jnp.exp(alr_ref[...]))
    adt_r = dt_r * (-jnp.exp(alc_ref[...]))
    rq = lax.broadcasted_iota(jnp.int32, (1, R), 1)
    lower = (((r // rps) == (rq // rps)) & (rq <= r)).astype(BF16) if rps != R else (rq <= r).astype(BF16)
    ac_c = _sel_dot(lower, adt_c)
    tr = lax.broadcasted_iota(jnp.int32, (CHUNK, 1), 0)
    upper = ((tr <= t) & (tr < R) & ((tr // rps) == (t // rps))).astype(BF16)
    ac_r = _dot_sel(adt_r, upper)

    last = [s * rps + rps - 1 if nseq > 1 else R - 1 for s in range(nseq)]
    ac_end = _seq_select(seq_c, nseq, lambda s: ac_c[last[s]:last[s] + 1, :])
    eh = lax.broadcasted_iota(jnp.int32, (LANES, W), 0)
    el = lax.broadcasted_iota(jnp.int32, (LANES, W), 1)
    expand = (eh == off + el // P).astype(BF16)
    stacked = jnp.concatenate([dt_c, jnp.exp(ac_c), jnp.exp(ac_end - ac_c)], axis=0)
    full = _dot_sel(stacked, expand)
    dt_full, dec_full, tail_full = full[:R], full[R:2 * R], full[2 * R:]
    xdt = x * dt_full
    xdt_b = _pad_rows(xdt.astype(BF16), CHUNK)
    xw = (xdt * tail_full).astype(BF16)

    for g in range(G):
        cg = cm[:, g * N:(g + 1) * N].astype(BF16)
        bg = bm[:, g * N:(g + 1) * N].astype(BF16)
        cb = _dot_nt(cg, _pad_rows(bg, CHUNK))
        gsl = slice(g * hg * P, (g + 1) * hg * P)
        y_inter = _seq_select(
            seq_c, nseq,
            lambda s: _dot_nt(cg, hs_ref[0, s, g * hg:(g + 1) * hg].reshape(hg * P, N).astype(BF16)))
        ys = []
        for hh in range(hg):
            h = g * hg + hh
            seg = jnp.exp(jnp.where(causal, ac_c[:, off + h:off + h + 1] - ac_r[off + h:off + h + 1, :], NEG))
            ys.append(_dot((cb * seg).astype(BF16), xdt_b[:, h * P:(h + 1) * P]))
        y_g = jnp.concatenate(ys, axis=1) + y_inter * dec_full[:, gsl]
        y_g = y_g + dsk_ref[:, gsl] * x[:, gsl]
        zf = z_ref[:, gsl].astype(F32)
        y_ref[:, gsl] = (y_g * (zf * _sigmoid(zf))).astype(y_ref.dtype)
        for s in range(nseq):
            xw_s = xw[:, gsl] if nseq == 1 else jnp.where(seq_c == s, xw[:, gsl], jnp.zeros((), BF16))
            upd = _dot_tn(xw_s, bg)
            for hh in range(hg):
                h = g * hg + hh
                dec = jnp.exp(ac_c[last[s]:last[s] + 1, off + h:off + h + 1])
                hs_ref[0, s, h] = dec * hs_ref[0, s, h] + upd[hh * P:(hh + 1) * P, :]


def _ssd(pc, pab, gates, gates_t, prev, conv_w, conv_b, dtb_r, dtb_c, al_r, al_c, dskip_full, h_all, layers,
         *, B, R, rps, nseq, nc, rb0, l_valid, zcol):
    T, CW = pc.shape
    nheads = h_all.shape[2]
    W = nheads * SSD_HEADDIM
    tile_rows = prev.shape[1]
    nb = B // nseq

    def rowmap(col):
        return lambda b, j: (rb0 + b * nc + j, col)

    def const(b, j):
        return (0, 0)

    body = functools.partial(_ssd_body, R=R, rps=rps, nseq=nseq, l_valid=l_valid, tile_rows=tile_rows)
    return _layered_state_call(
        body, (nb, nc),
        [pl.BlockSpec((R, CW), rowmap(0)), pl.BlockSpec((R, W), rowmap(zcol)),
         pl.BlockSpec((R, LANES), rowmap(0)),
         pl.BlockSpec((1, LANES, CHUNK), lambda b, j: (b * nc + j, 0, 0)),
         pl.BlockSpec((nseq, tile_rows, CW), lambda b, j: (b, 0, 0)),
         pl.BlockSpec((CONV_W, CW), const), pl.BlockSpec((1, CW), const),
         pl.BlockSpec((1, LANES), const), pl.BlockSpec((LANES, 1), const),
         pl.BlockSpec((1, LANES), const), pl.BlockSpec((LANES, 1), const),
         pl.BlockSpec((1, W), const)],
        [pc, pab, gates, gates_t, prev, conv_w, conv_b, dtb_r, dtb_c, al_r, al_c, dskip_full],
        [pl.BlockSpec((R, W), lambda b, j: (b * nc + j, 0))],
        [jax.ShapeDtypeStruct((nb * nc * R, W), BF16)],
        1, h_all, (nseq, nheads, SSD_HEADDIM, SSD_STATE), layers, "ssd",
        scratch_shapes=[pltpu.VMEM((nseq * tile_rows, CW), F32)])


def _lru_body(xl_ref, yl_ref, prev_ref, h0_ref, cw_ref, cb_ref, wa_ref, ba_ref, wi_ref, bi_ref,
              lam_ref, o_ref, hs_ref, prev_s, *, R, rps, nseq, l_valid, tile_rows):
    W = xl_ref.shape[1]
    bw = W // LRU_BLOCKS
    j = pl.program_id(1)

    @pl.when(j == 0)
    def _():
        hs_ref[...] = h0_ref[...]
        prev_s[...] = prev_ref[...].reshape(prev_s.shape)

    r, t, seq_c, seq_r, pos_c, pos_r = _row_ids(R, rps, R)
    valid_c = pos_c < l_valid
    u = xl_ref[...]
    taps = _conv_shift(u, prev_s[...], R, rps, tile_rows)
    if nseq == 1:
        prev_s[...] = u[R - tile_rows:, :].astype(F32)
    xl = cb_ref[...] + taps[0] * cw_ref[CONV_W - 1:CONV_W, :]
    for jj in range(1, CONV_W):
        xl = xl + taps[jj] * cw_ref[CONV_W - 1 - jj:CONV_W - jj, :]
    xb = xl.astype(BF16)
    ra = jnp.concatenate([_dot(xb[:, n * bw:(n + 1) * bw], wa_ref[n]) for n in range(LRU_BLOCKS)], axis=1)
    ri = jnp.concatenate([_dot(xb[:, n * bw:(n + 1) * bw], wi_ref[n]) for n in range(LRU_BLOCKS)], axis=1)
    rg = _sigmoid(ra + ba_ref[...])
    ig = _sigmoid(ri + bi_ref[...])
    log_a = -LRU_C * rg * _softplus(-lam_ref[...])
    a = jnp.where(valid_c, jnp.exp(log_a), 1.0)
    th = jnp.tanh(log_a)
    bv = jnp.where(valid_c, jnp.sqrt(-2.0 * th / (1.0 - th)) * (ig * xl), 0.0)
    i_in = r % rps if rps != R else r
    step = 1
    while step < rps:
        a_sh = pltpu.roll(a, step, 0)
        b_sh = pltpu.roll(bv, step, 0)
        m = i_in >= step
        bv = jnp.where(m, a * b_sh + bv, bv)
        a = jnp.where(m, a * a_sh, a)
        step *= 2
    h0 = _seq_select(seq_c, nseq, lambda s: hs_ref[s])
    h_seq = a * h0 + bv
    for s in range(nseq):
        lr = s * rps + rps - 1 if nseq > 1 else R - 1
        hs_ref[s] = h_seq[lr:lr + 1, :]
    o_ref[...] = (h_seq * _gelu_tanh(yl_ref[...].astype(F32))).astype(o_ref.dtype)


def _lru(pab, prev, h0, conv_w, conv_b, wa, ba, wi, bi, lam, *, B, R, rps, nseq, nc, rb0, l_valid,
         xcol, ycol):
    T = pab.shape[0]
    W = lam.shape[1]
    bw = W // LRU_BLOCKS
    tile_rows = prev.shape[1]
    nb = B // nseq

    def rowmap(col):
        return lambda b, j: (rb0 + b * nc + j, col)

    def const(b, j):
        return (0, 0)

    def const3(b, j):
        return (0, 0, 0)

    body = functools.partial(_lru_body, R=R, rps=rps, nseq=nseq, l_valid=l_valid, tile_rows=tile_rows)
    st = pl.BlockSpec((nseq, 1, W), lambda b, j: (b, 0, 0))
    vec = pl.BlockSpec((1, W), const)
    return pl.pallas_call(
        body, grid=(nb, nc),
        in_specs=[pl.BlockSpec((R, W), rowmap(xcol)), pl.BlockSpec((R, W), rowmap(ycol)),
                  pl.BlockSpec((nseq, tile_rows, W), lambda b, j: (b, 0, 0)), st,
                  pl.BlockSpec((CONV_W, W), const), vec,
                  pl.BlockSpec((LRU_BLOCKS, bw, bw), const3), vec,
                  pl.BlockSpec((LRU_BLOCKS, bw, bw), const3), vec, vec],
        out_specs=[pl.BlockSpec((R, W), lambda b, j: (b * nc + j, 0)), st],
        out_shape=[jax.ShapeDtypeStruct((nb * nc * R, W), BF16), jax.ShapeDtypeStruct(h0.shape, F32)],
        scratch_shapes=[pltpu.VMEM((nseq * tile_rows, W), F32)],
        compiler_params=_cparams(("parallel", "arbitrary")), name="lru",
    )(pab, pab, prev, h0, conv_w, conv_b, wa, ba, wi, bi, lam)


def _outproj_body(hmp_ref, hsp_ref, hlp_ref, hms_ref, hss_ref, hls_ref, x_ref, wm_ref, ws_ref, wl_ref,
                  gs_ref, g_ref, b_ref, wr_ref, br_ref, x1_ref, x1b_ref, idx_ref, gate_ref, *, alpha, npb):
    is_p = pl.program_id(0) < npb
    hm = jnp.where(is_p, hmp_ref[...], hms_ref[...])
    hl = jnp.where(is_p, hlp_ref[...], hls_ref[...])
    ysf = jnp.where(is_p, hsp_ref[...], hss_ref[...]).astype(F32)
    ysn = ysf * lax.rsqrt(jnp.mean(ysf * ysf, axis=1, keepdims=True) + RMS_EPS) * gs_ref[...]
    y = _dot(hm, wm_ref[...]) + _dot(ysn.astype(BF16), ws_ref[...]) + _dot(hl, wl_ref[...])
    x1 = _layer_norm(alpha * x_ref[...] + y, g_ref[...], b_ref[...])
    x1b = x1.astype(BF16)
    x1_ref[...] = x1
    x1b_ref[...] = x1b
    logits = _dot(x1b, wr_ref[...]) + br_ref[...]
    lane = lax.broadcasted_iota(jnp.int32, logits.shape, 1)
    idx_out = jnp.zeros(logits.shape, jnp.int32)
    val_out = jnp.zeros(logits.shape, F32)
    top = None
    for k in range(TOP_K):
        mx = jnp.max(logits, axis=1, keepdims=True)
        sel = jnp.min(jnp.where(logits == mx, lane, LANES), axis=1, keepdims=True)
        if top is None:
            top = mx
        idx_out = jnp.where(lane == k, sel, idx_out)
        val_out = jnp.where(lane == k, jnp.exp(mx - top), val_out)
        logits = jnp.where(lane == sel, -jnp.inf, logits)
    gate_ref[...] = val_out / jnp.sum(val_out, axis=1, keepdims=True)
    idx_ref[...] = idx_out


def _outproj(mix_p, mix_s, x, wm, ws, wl, g_ssd, ln_g, ln_b, wr, br, alpha):
    T, D = x.shape
    tm = OUT_TILE
    npb = mix_p[0].shape[0] // tm
    assert npb * tm == mix_p[0].shape[0] and mix_s[0].shape[0] % tm == 0

    def rows(w):
        return pl.BlockSpec((tm, w), lambda i: (i, 0))

    def rows_p(a):
        return pl.BlockSpec((tm, a.shape[1]), lambda i: (jnp.minimum(i, npb - 1), 0))

    def rows_s(a):
        return pl.BlockSpec((tm, a.shape[1]), lambda i: (jnp.maximum(i - npb, 0), 0))

    def whole(a):
        return pl.BlockSpec(a.shape, lambda i: (0, 0), pipeline_mode=pl.Buffered(1))

    return pl.pallas_call(
        functools.partial(_outproj_body, alpha=alpha, npb=npb), grid=(T // tm,),
        in_specs=[rows_p(mix_p[0]), rows_p(mix_p[1]), rows_p(mix_p[2]),
                  rows_s(mix_s[0]), rows_s(mix_s[1]), rows_s(mix_s[2]), rows(D),
                  whole(wm), whole(ws), whole(wl), whole(g_ssd), whole(ln_g), whole(ln_b),
                  whole(wr), whole(br)],
        out_specs=[rows(D), rows(D), rows(LANES), rows(LANES)],
        out_shape=[jax.ShapeDtypeStruct((T, D), F32), jax.ShapeDtypeStruct((T, D), BF16),
                   jax.ShapeDtypeStruct((T, LANES), jnp.int32), jax.ShapeDtypeStruct((T, LANES), F32)],
        compiler_params=_cparams(("parallel",)), name="outproj_ln_router",
    )(*mix_p, *mix_s, x, wm, ws, wl, g_ssd, ln_g, ln_b, wr, br)


def _expert_body(ie_ref, ir0_ref, ins_ref, inv_ref, na_ref, tok_ref, dst_ref,
                 x_hbm, wg_ref, bg_ref, wu_ref, bu_ref, wd_ref, bd_ref, yg_in, yg_hbm,
                 xbuf, acc, outbuf, sem_in, sem_out, *, n_items, nf):
    del yg_in
    i = pl.program_id(0)
    j = pl.program_id(1)
    slot = i % 2
    half = wg_ref.shape[2] // 2

    def row_in(it_slot, tok, r):
        return pltpu.make_async_copy(x_hbm.at[pl.ds(tok, 1)], xbuf.at[it_slot, pl.ds(r, 1)], sem_in.at[it_slot])

    def row_out(r, dst):
        return pltpu.make_async_copy(outbuf.at[pl.ds(r, 1)], yg_hbm.at[pl.ds(dst, 1)], sem_out.at[0])

    def start_gather(it, it_slot):
        r0 = ir0_ref[it]

        def go(g, c):
            for u in range(DMA_UNROLL):
                r = g * DMA_UNROLL + u
                row_in(it_slot, tok_ref[r0 + r], r).start()
            return c
        lax.fori_loop(0, ins_ref[it] * (MOE_SUB // DMA_UNROLL), go, 0)

    def wait_gather(it, it_slot):
        def go(g, c):
            for u in range(DMA_UNROLL):
                row_in(it_slot, 0, 0).wait()
            return c
        lax.fori_loop(0, ins_ref[it] * (MOE_SUB // DMA_UNROLL), go, 0)

    @pl.when(j == 0)
    def _():
        @pl.when(i == 0)
        def _():
            xbuf[...] = jnp.zeros_like(xbuf)
            start_gather(0, 0)

        for static_slot in range(2):
            @pl.when(slot == static_slot)
            def _():
                wait_gather(i, static_slot)

                @pl.when(i + 1 < n_items)
                def _():
                    start_gather(i + 1, 1 - static_slot)

    r0 = ir0_ref[i]
    n_valid = inv_ref[i]
    n_sub = ins_ref[i]
    d_model = wd_ref.shape[3]

    def compute(m):
        wg = wg_ref[0, 0].astype(BF16)
        wu = wu_ref[0, 0].astype(BF16)
        wd = wd_ref[0, 0].astype(BF16)
        w = xbuf[slot, :m, :]
        lo = lax.bitcast_convert_type(w << 16, F32).astype(BF16)
        hi = lax.bitcast_convert_type(w & jnp.uint32(0xFFFF0000), F32).astype(BF16)
        gate = jnp.minimum(_dot(lo, wg[:half]) + _dot(hi, wg[half:]) + bg_ref[0, 0], SWIGLU_LIMIT)
        up = jnp.clip(_dot(lo, wu[:half]) + _dot(hi, wu[half:]) + bu_ref[0, 0], -SWIGLU_LIMIT, SWIGLU_LIMIT)
        act = ((up + 1.0) * gate * _sigmoid(SWIGLU_ALPHA * gate)).astype(BF16)

        def down(dst_ref_, add_acc, add_bias):
            for n in range(d_model // MOE_TN):
                cols = slice(n * MOE_TN, (n + 1) * MOE_TN)
                part = _dot(act, wd[:, cols])
                if add_acc:
                    part = part + acc[:m, cols]
                if add_bias:
                    part = part + bd_ref[0, 0, :, cols]
                dst_ref_[:m, cols] = part

        if nf == 1:
            down(outbuf, False, True)
        else:
            pl.when(j == 0)(lambda: down(acc, False, True))
            pl.when((j > 0) & (j < nf - 1))(lambda: down(acc, True, False))
            pl.when(j == nf - 1)(lambda: down(outbuf, True, False))

    small = MOE_NSUB // 2 + 1

    @pl.when(j == nf - 1)
    def _():
        def wait8(g, c):
            for u in range(DMA_UNROLL):
                row_out(0, 0).wait()
            return c

        def wait1(q, c):
            row_out(0, 0).wait()
            return c
        n_prev = jnp.where(i > 0, inv_ref[jnp.maximum(i - 1, 0)], 0)
        lax.fori_loop(0, n_prev // DMA_UNROLL, wait8, 0)
        lax.fori_loop(n_prev // DMA_UNROLL * DMA_UNROLL, n_prev, wait1, 0)

    @pl.when((n_sub > 0) & (n_sub <= small))
    def _():
        compute(small * MOE_SUB)

    @pl.when(n_sub > small)
    def _():
        compute(MOE_NSUB * MOE_SUB)

    @pl.when(j == nf - 1)
    def _():
        def go8(g, c):
            for u in range(DMA_UNROLL):
                r = g * DMA_UNROLL + u
                row_out(r, dst_ref[r0 + r]).start()
            return c
        lax.fori_loop(0, n_valid // DMA_UNROLL, go8, 0)

        def go1(r, c):
            row_out(r, dst_ref[r0 + r]).start()
            return c
        lax.fori_loop(n_valid // DMA_UNROLL * DMA_UNROLL, n_valid, go1, 0)

        @pl.when(i == n_items - 1)
        def _():
            def wait1(q, c):
                row_out(0, 0).wait()
                return c
            lax.fori_loop(0, n_valid, wait1, 0)


def _experts(xw, items, tok, dst, wg, bg, wu, bu, wd, bd, yg, l):
    Fh = wg.shape[3]
    D = wg.shape[2]
    item_e, item_r0, item_ns, item_nv, n_act = items
    n_items = item_e.shape[0]
    nf = Fh // MOE_TF
    cap = MOE_NSUB * MOE_SUB

    def jeff(i, j, na):
        return jnp.where(i < na[0], j, nf - 1)

    def wmap(i, j, ie, ir0, ins, inv, na, tk, ds):
        return (l, ie[i], 0, jeff(i, j, na))

    def wdmap(i, j, ie, ir0, ins, inv, na, tk, ds):
        return (l, ie[i], jeff(i, j, na), 0)

    def bdmap(i, j, ie, ir0, ins, inv, na, tk, ds):
        return (l, ie[i], 0, 0)

    grid_spec = pltpu.PrefetchScalarGridSpec(
        num_scalar_prefetch=7, grid=(n_items, nf),
        in_specs=[pl.BlockSpec(memory_space=pl.ANY),
                  pl.BlockSpec((1, 1, D, MOE_TF), wmap), pl.BlockSpec((1, 1, 1, MOE_TF), wmap),
                  pl.BlockSpec((1, 1, D, MOE_TF), wmap), pl.BlockSpec((1, 1, 1, MOE_TF), wmap),
                  pl.BlockSpec((1, 1, MOE_TF, D), wdmap), pl.BlockSpec((1, 1, 1, D), bdmap),
                  pl.BlockSpec(memory_space=pl.ANY)],
        out_specs=pl.BlockSpec(memory_space=pl.ANY),
        scratch_shapes=[pltpu.VMEM((2, cap, D // 2), jnp.uint32), pltpu.VMEM((cap, D), F32),
                        pltpu.VMEM((cap, D), F32), pltpu.SemaphoreType.DMA((2,)),
                        pltpu.SemaphoreType.DMA((1,))])
    return pl.pallas_call(
        functools.partial(_expert_body, n_items=n_items, nf=nf), grid_spec=grid_spec,
        out_shape=jax.ShapeDtypeStruct(yg.shape, F32), input_output_aliases={14: 0},
        compiler_params=_cparams(("arbitrary", "arbitrary")), name="moe_experts",
    )(item_e, item_r0, item_ns, item_nv, n_act, tok, dst, xw, wg, bg, wu, bu, wd, bd, yg)


def _combine_body(yg_ref, gate_ref, x_ref, g_ref, b_ref, o_ref, ob_ref, *, alpha):
    gates = gate_ref[...]
    f = gates[:, 0:1] * yg_ref[0]
    for k in range(1, TOP_K):
        f = f + gates[:, k:k + 1] * yg_ref[k]
    x2 = _layer_norm(alpha * x_ref[...] + f, g_ref[...], b_ref[...])
    o_ref[...] = x2
    ob_ref[...] = x2.astype(BF16)


def _combine_ln(yg, gates, x, ln_g, ln_b, alpha):
    T, D = x.shape
    tm = OUT_TILE
    rows = pl.BlockSpec((tm, D), lambda i: (i, 0))
    vec = pl.BlockSpec((1, D), lambda i: (0, 0))
    return pl.pallas_call(
        functools.partial(_combine_body, alpha=alpha), grid=(T // tm,),
        in_specs=[pl.BlockSpec((TOP_K, tm, D), lambda i: (0, i, 0)),
                  pl.BlockSpec((tm, LANES), lambda i: (i, 0)), rows, vec, vec],
        out_specs=[rows, rows],
        out_shape=[jax.ShapeDtypeStruct((T, D), F32), jax.ShapeDtypeStruct((T, D), BF16)],
        compiler_params=_cparams(("parallel",)), name="moe_combine_ln",
    )(yg, gates, x, ln_g, ln_b)


def _moe_dispatch(idx, valid_tok, n_experts, n_rows, n_items):
    T = idx.shape[0]
    cap = MOE_NSUB * MOE_SUB
    flat_e = jnp.where(valid_tok[:, None], idx[:, :TOP_K], n_experts).reshape(-1)
    order = jnp.argsort(flat_e)
    counts = jnp.bincount(flat_e, length=n_experts + 1)[:n_experts]
    nsub = (counts + MOE_SUB - 1) // MOE_SUB
    pad_end = jnp.cumsum(nsub * MOE_SUB)
    pad_start = pad_end - nsub * MOE_SUB
    start = jnp.cumsum(counts) - counts
    nit = (nsub + MOE_NSUB - 1) // MOE_NSUB
    it_end = jnp.cumsum(nit)
    it_start = it_end - nit
    n_act = it_end[-1]
    t = jnp.arange(n_items)
    e_t = jnp.minimum(jnp.sum(t[:, None] >= it_end[None, :], axis=1), n_experts - 1)
    k = t - it_start[e_t]
    active = t < n_act
    e_last = e_t[jnp.maximum(n_act - 1, 0)]
    item_e = jnp.where(active, e_t, e_last)
    item_r0 = jnp.where(active, pad_start[e_t] + k * cap, 0)
    item_ns = jnp.where(active, jnp.minimum(nsub[e_t] - k * MOE_NSUB, MOE_NSUB), 0)
    item_nv = jnp.where(active, jnp.clip(counts[e_t] - k * cap, 0, cap), 0)
    items = tuple(a.astype(jnp.int32) for a in (item_e, item_r0, item_ns, item_nv, n_act.reshape(1)))
    p = jnp.arange(n_rows)
    row_e = jnp.minimum(jnp.sum(p[:, None] >= pad_end[None, :], axis=1), n_experts - 1)
    off = p - pad_start[row_e]
    valid = (off < counts[row_e]) & (p < pad_end[-1])
    a = order[jnp.where(valid, start[row_e] + off, 0)]
    tok = jnp.where(valid, a // TOP_K, 0).astype(jnp.int32)
    dst = jnp.where(valid, (a % TOP_K) * T + a // TOP_K, -1).astype(jnp.int32)
    return items, tok, dst


def kernel(x_prompt, x_sample, state_mlstm_C, state_mlstm_n, state_mlstm_m, state_ssd_h, state_ssd_conv,
           state_lru_h, state_lru_conv, meta, ln_in_g, ln_in_b, w_in, b_if, conv_ssd_w, conv_ssd_b, dt_bias,
           a_log, d_skip, g_ssd, g_mlstm, conv_lru_w, conv_lru_b, w_rg_a, b_rg_a, w_rg_i, b_rg_i, lam, w_out,
           ln1_g, ln1_b, w_router, b_router, w_gate, b_gate, w_up, b_up, w_down, b_down, ln2_g, ln2_b):
    Bp, S, D = x_prompt.shape
    Bs, Ls, _ = x_sample.shape
    depth = w_in.shape[0]
    n_meta = meta.shape[0]
    nh = MLSTM_HEADS
    MW = g_mlstm.shape[1]
    SW = g_ssd.shape[1]
    LW = lam.shape[1]
    CW = conv_ssd_w.shape[2]
    n_ssd_heads = dt_bias.shape[1]
    E = w_router.shape[2]
    alpha = (2 * depth) ** 0.25
    Lp = n_meta + S
    LP = -(-Lp // CHUNK) * CHUNK
    LS = SAMPLE_LS
    Tp, Ts = Bp * LP, Bs * LS
    T = Tp + Ts
    ncp = LP // CHUNK
    RS = SAMPLE_SEQS * LS
    assert CONV_W - 1 <= Ls <= LS and T % ROW_TILE == 0 and T % OUT_TILE == 0 and Bs % SAMPLE_SEQS == 0 and Tp % RS == 0

    xp = jnp.concatenate([jnp.broadcast_to(meta.astype(F32)[None], (Bp, n_meta, D)), x_prompt,
                          jnp.zeros((Bp, LP - Lp, D), F32)], axis=1).reshape(Tp, D)
    xs = jnp.concatenate([x_sample, jnp.zeros((Bs, LS - Ls, D), F32)], axis=1).reshape(Ts, D)
    x, xb = _ln_in(jnp.concatenate([xp, xs], axis=0), ln_in_g, ln_in_b)
    pos = np.concatenate([np.tile(np.arange(LP) < Lp, Bp), np.tile(np.arange(LS) < Ls, Bs)])
    valid_tok = jnp.asarray(pos)
    tail_p = (np.arange(Bp)[:, None] * LP + np.arange(Lp - (CONV_W - 1), Lp)[None, :]).reshape(-1)
    tail_s = (Tp + np.arange(Bs)[:, None] * LS + np.arange(Ls - (CONV_W - 1), Ls)[None, :]).reshape(-1)

    sizes = [MW, MW, MW, MW, nh, nh, SW, CW, n_ssd_heads, LW, LW]
    o = [0] + [int(v) for v in np.cumsum(sizes)]
    w_ab = jnp.concatenate([w_in[:, :, o[0]:o[4]], w_in[:, :, o[6]:o[7]], w_in[:, :, o[9]:o[11]]],
                           axis=2).astype(BF16)
    w_c = w_in[:, :, o[7]:o[8]].astype(BF16)
    n_gate = 2 * nh + n_ssd_heads
    w_d = jnp.concatenate([w_in[:, :, o[4]:o[6]], w_in[:, :, o[8]:o[9]],
                           jnp.zeros((depth, D, LANES - n_gate), F32)], axis=2).astype(BF16)
    zcol = (4 * MW) // SW
    xcol = (4 * MW + SW) // LW
    ycol = xcol + 1
    assert zcol * SW == 4 * MW and xcol * LW == 4 * MW + SW

    def lane_pad(v, fill=0.0, front=0):
        return jnp.concatenate([jnp.full((front,), fill, F32), v.astype(F32),
                                jnp.full((LANES - front - v.shape[0],), fill, F32)])

    w_out_b = w_out.astype(BF16)
    wr_b = jnp.concatenate([w_router, jnp.zeros((depth, D, LANES - E), F32)], axis=2).astype(BF16)
    n_assign = (Bp * Lp + Bs * Ls) * TOP_K
    n_rows = -(-(n_assign + E * (MOE_SUB - 1)) // MOE_SUB) * MOE_SUB
    n_items = E + -(-(n_rows // MOE_SUB) // MOE_NSUB)
    yg = jnp.zeros((TOP_K * T, D), F32)

    zeros_p = dict(
        c=jnp.zeros((1, Bp, nh, MW // nh, MW // nh), F32), n=jnp.zeros((Bp, nh, MW // nh), F32),
        m=jnp.zeros((Bp, 1, nh), F32), h=jnp.zeros((1, Bp, n_ssd_heads, SSD_HEADDIM, SSD_STATE), F32),
        cs=jnp.zeros((Bp, 16, CW), F32), l=jnp.zeros((Bp, 1, LW), F32), cl=jnp.zeros((Bp, 16, LW), F32))
    pgrp = dict(B=Bp, R=CHUNK, rps=CHUNK, nseq=1, nc=ncp, rb0=0, l_valid=Lp)
    sgrp = dict(B=Bs, R=RS, rps=LS, nseq=SAMPLE_SEQS, nc=1, rb0=Tp // RS, l_valid=Ls)

    outs = [[] for _ in range(14)]
    bg4, bu4, bd4 = (b.reshape(depth, E, 1, b.shape[2]) for b in (b_gate, b_up, b_down))
    s_c_all = jnp.zeros(state_mlstm_C.shape, F32)
    s_h_all = jnp.zeros(state_ssd_h.shape, F32)
    for l in range(depth):
        pab = _matmul(xb, w_ab, l, BF16, 1024, "inproj_ab")
        pc = _matmul(xb, w_c, l, BF16, CW // 2, "inproj_c")
        gates = _matmul(xb, w_d, l, F32, LANES, "inproj_gates")
        gt_p = gates[:Tp].reshape(Bp * ncp, CHUNK, LANES).transpose(0, 2, 1)
        gt_s = gates[Tp:].reshape(Bs // SAMPLE_SEQS, RS, LANES).transpose(0, 2, 1)
        gt_s = jnp.concatenate([gt_s, jnp.zeros((Bs // SAMPLE_SEQS, LANES, CHUNK - RS), F32)], axis=2)

        bif_r = lane_pad(b_if[l]).reshape(1, LANES)
        bif_c = bif_r.reshape(LANES, 1)
        dtb_r = lane_pad(dt_bias[l], front=2 * nh).reshape(1, LANES)
        al_r = lane_pad(a_log[l], front=2 * nh).reshape(1, LANES)
        dsk = jnp.repeat(d_skip[l].astype(F32), SSD_HEADDIM).reshape(1, SW)
        gm = g_mlstm[l].reshape(1, MW).astype(F32)
        cw_s, cb_s = conv_ssd_w[l].astype(F32), conv_ssd_b[l].reshape(1, CW).astype(F32)
        cw_l, cb_l = conv_lru_w[l].astype(F32), conv_lru_b[l].reshape(1, LW).astype(F32)
        wa, wi = w_rg_a[l].astype(BF16), w_rg_i[l].astype(BF16)
        ba, bi = b_rg_a[l].reshape(1, LW).astype(F32), b_rg_i[l].reshape(1, LW).astype(F32)
        lam_l = lam[l].reshape(1, LW).astype(F32)

        def pad_conv(buf):
            return jnp.concatenate([jnp.zeros((buf.shape[0], SUBLANES - (CONV_W - 1), buf.shape[2]), F32),
                                    buf.astype(F32)], axis=1)

        st_s = dict(c=state_mlstm_C.astype(F32), n=state_mlstm_n[l].astype(F32),
                    m=state_mlstm_m[l].astype(F32).reshape(Bs, 1, nh), h=state_ssd_h.astype(F32),
                    cs=pad_conv(state_ssd_conv[l]), l=state_lru_h[l].astype(F32).reshape(Bs, 1, LW),
                    cl=pad_conv(state_lru_conv[l]))

        mixes, new_states = [], []
        for gi, (grp, st, gt) in enumerate(((pgrp, zeros_p, gt_p), (sgrp, st_s, gt_s))):
            lay_c = (0, 0, 1, None) if gi == 0 else (l, l, depth, s_c_all)
            lay_h = (0, 0, 1, None) if gi == 0 else (l, l, depth, s_h_all)
            hm_g, c_n, n_n, m_n = _mlstm(pab, gates, gt, bif_r, bif_c, gm, st["n"], st["m"], st["c"], lay_c, **grp)
            hs_g, h_n = _ssd(pc, pab, gates, gt, st["cs"], cw_s, cb_s, dtb_r, dtb_r.reshape(LANES, 1),
                             al_r, al_r.reshape(LANES, 1), dsk, st["h"], lay_h, zcol=zcol, **grp)
            if gi == 1:
                s_c_all, s_h_all = c_n, h_n
            hl_g, l_n = _lru(pab, st["cl"], st["l"], cw_l, cb_l, wa, ba, wi, bi, lam_l, xcol=xcol, ycol=ycol,
                             **grp)
            new_states.append((c_n, n_n, m_n, h_n, l_n))
            mixes.append((hm_g, hs_g, hl_g))

        x1, x1b, idx, gate_w = _outproj(
            mixes[0], mixes[1], x, w_out_b[l, :MW], w_out_b[l, MW:MW + SW], w_out_b[l, MW + SW:],
            g_ssd[l].reshape(1, SW).astype(F32), ln1_g[l].reshape(1, D), ln1_b[l].reshape(1, D),
            wr_b[l], lane_pad(b_router[l], fill=NEG).reshape(1, LANES), alpha)

        items, row_tok, row_dst = _moe_dispatch(idx, valid_tok, E, n_rows, n_items)
        bits = lax.bitcast_convert_type(x1b, jnp.uint16).astype(jnp.uint32)
        xw = bits[:, :D // 2] | (bits[:, D // 2:] << 16)
        yg = _experts(xw, items, row_tok, row_dst, w_gate, bg4, w_up, bu4, w_down, bd4, yg, l)
        x, xb = _combine_ln(yg.reshape(TOP_K, T, D), gate_w, x1, ln2_g[l].reshape(1, D), ln2_b[l].reshape(1, D),
                            alpha)

        xbc_p = pc[tail_p].reshape(Bp, CONV_W - 1, CW).astype(F32)
        xbc_s = pc[tail_s].reshape(Bs, CONV_W - 1, CW).astype(F32)
        xl_p = pab[tail_p][:, xcol * LW:(xcol + 1) * LW].reshape(Bp, CONV_W - 1, LW).astype(F32)
        xl_s = pab[tail_s][:, xcol * LW:(xcol + 1) * LW].reshape(Bs, CONV_W - 1, LW).astype(F32)
        for gi, (bsz, conv_s, conv_l) in enumerate(((Bp, xbc_p, xl_p), (Bs, xbc_s, xl_s))):
            c_n, n_n, m_n, h_n, l_n = new_states[gi]
            vals = (c_n[0], n_n, m_n.reshape(bsz, nh), h_n[0], conv_s, l_n.reshape(bsz, LW), conv_l)
            for k, v in enumerate(vals):
                if gi == 0 or k not in (0, 3):
                    outs[gi * 7 + k].append(v)

    y_prompt = x[:Tp].reshape(Bp, LP, D)[:, n_meta:Lp]
    y_sample = x[Tp:].reshape(Bs, LS, D)[:, :Ls]
    state_dtypes = [state_mlstm_C.dtype, state_mlstm_n.dtype, state_mlstm_m.dtype, state_ssd_h.dtype,
                    state_ssd_conv.dtype, state_lru_h.dtype, state_lru_conv.dtype]
    stacked = [jnp.stack(v).astype(state_dtypes[k % 7]) if v else None for k, v in enumerate(outs)]
    stacked[7] = s_c_all.astype(state_dtypes[0])
    stacked[7 + 3] = s_h_all.astype(state_dtypes[3])
    return (y_prompt, y_sample, *stacked)
```

```python
import functools
import math

import numpy as np
import jax
import jax.numpy as jnp
from jax import lax
from jax.experimental import pallas as pl
from jax.experimental.pallas import tpu as pltpu

F32 = jnp.float32
BF16 = jnp.bfloat16

LANES = 128
SUBLANES = 8
VMEM_LIMIT = 56 * 1024 * 1024

CONV_W = 4
MLSTM_HEADS = 4
SSD_HEADDIM = 64
SSD_STATE = 128
SSD_GROUPS = 2
LRU_BLOCKS = 8
LRU_C = 8.0
TOP_K = 4
SWIGLU_LIMIT = 7.0
SWIGLU_ALPHA = 1.702
LN_EPS = 1e-5
RMS_EPS = 1e-6
NEG = -1e30

CHUNK = 128
SAMPLE_LS = 8
SAMPLE_SEQS = 4
ROW_TILE = 512
MOE_SUB = 256
MOE_NSUB = 5
MOE_TF = 256
MOE_TN = 512
DMA_UNROLL = 8
OUT_TILE = 256


def _cparams(sem):
    return pltpu.CompilerParams(dimension_semantics=sem, vmem_limit_bytes=VMEM_LIMIT)


def _softplus(x):
    return jnp.maximum(x, 0.0) + jnp.log1p(jnp.exp(-jnp.abs(x)))


def _log_sigmoid(x):
    return -_softplus(-x)


def _sigmoid(x):
    return 1.0 / (1.0 + jnp.exp(-x))


def _gelu_tanh(x):
    return 0.5 * x * (1.0 + jnp.tanh(math.sqrt(2.0 / math.pi) * (x + 0.044715 * (x * x * x))))


def _layer_norm(x, g, b):
    mu = jnp.mean(x, axis=-1, keepdims=True)
    xc = x - mu
    var = jnp.mean(xc * xc, axis=-1, keepdims=True)
    return xc * lax.rsqrt(var + LN_EPS) * g + b


def _dot(a, b, precision=None):
    return jnp.dot(a, b, preferred_element_type=F32, precision=precision)


def _split_bf16(x):
    hi = x.astype(BF16)
    return hi, (x - hi.astype(F32)).astype(BF16)


def _dot_sel(x, sel):
    hi, lo = _split_bf16(x)
    return _dot(hi, sel) + _dot(lo, sel)


def _sel_dot(sel, x):
    hi, lo = _split_bf16(x)
    return _dot(sel, hi) + _dot(sel, lo)


def _dot_nt(a, b):
    return lax.dot_general(a, b, (((1,), (1,)), ((), ())), preferred_element_type=F32)


def _dot_tn(a, b):
    return lax.dot_general(a, b, (((0,), (0,)), ((), ())), preferred_element_type=F32)


def _ln_in_body(x_ref, g_ref, b_ref, o_ref, ob_ref):
    y = _layer_norm(x_ref[...], g_ref[...], b_ref[...])
    o_ref[...] = y
    ob_ref[...] = y.astype(BF16)


def _ln_in(x, g, b):
    T, D = x.shape
    row = pl.BlockSpec((ROW_TILE, D), lambda i: (i, 0))
    vec = pl.BlockSpec((1, D), lambda i: (0, 0))
    return pl.pallas_call(
        _ln_in_body, grid=(T // ROW_TILE,), in_specs=[row, vec, vec], out_specs=[row, row],
        out_shape=[jax.ShapeDtypeStruct((T, D), F32), jax.ShapeDtypeStruct((T, D), BF16)],
        compiler_params=_cparams(("parallel",)), name="ln_in")(x, g.reshape(1, D), b.reshape(1, D))


def _mm_body(x_ref, w_ref, o_ref):
    o_ref[...] = _dot(x_ref[...], w_ref[0]).astype(o_ref.dtype)


def _matmul(x, w, l, out_dtype, tn, name):
    T, K = x.shape
    N = w.shape[2]
    return pl.pallas_call(
        _mm_body, grid=(N // tn, T // ROW_TILE),
        in_specs=[pl.BlockSpec((ROW_TILE, K), lambda j, i: (i, 0)),
                  pl.BlockSpec((1, K, tn), lambda j, i: (l, 0, j))],
        out_specs=pl.BlockSpec((ROW_TILE, tn), lambda j, i: (i, j)),
        out_shape=jax.ShapeDtypeStruct((T, N), out_dtype),
        compiler_params=_cparams(("parallel", "parallel")), name=name)(x, w)


def _row_ids(R, rps, chunk_rows):
    r = lax.broadcasted_iota(jnp.int32, (R, 1), 0)
    t = lax.broadcasted_iota(jnp.int32, (1, CHUNK), 1)
    if rps == R:
        j = pl.program_id(1)
        return r, t, jnp.zeros_like(r), jnp.zeros_like(t), r + j * chunk_rows, t + j * chunk_rows
    return r, t, r // rps, t // rps, r % rps, t % rps


def _conv_shift(u_bf, prev_f32, R, rps, tile_rows):
    P = prev_f32.shape[0]
    r = lax.broadcasted_iota(jnp.int32, (R, 1), 0)
    ku = lax.broadcasted_iota(jnp.int32, (1, R), 1)
    kp = lax.broadcasted_iota(jnp.int32, (1, P), 1)
    i = r % rps
    seq = r // rps
    taps = [u_bf.astype(F32)]
    for j in range(1, CONV_W):
        su = ((i >= j) & (ku == r - j)).astype(BF16)
        sp = ((i < j) & (kp == seq * tile_rows + (tile_rows - j) + i)).astype(BF16)
        taps.append(_dot(su, u_bf) + _sel_dot(sp, prev_f32))
    return taps


def _seq_select(seq_c, nseq, fn):
    if nseq == 1:
        return fn(0)
    out = None
    for s in range(nseq):
        term = jnp.where(seq_c == s, fn(s), 0.0)
        out = term if out is None else out + term
    return out


def _pad_rows(x, rows):
    if x.shape[0] == rows:
        return x
    return jnp.concatenate([x, jnp.zeros((rows - x.shape[0],) + x.shape[1:], x.dtype)], axis=0)


def _mlstm_body(q_ref, k_ref, v_ref, o_ref, g_ref, gt_ref, bifr_ref, bifc_ref, gm_ref,
                n0_ref, m0_ref, c0_ref, h_ref, c_ref, n_ref, m_ref, *, R, rps, nseq, l_valid):
    nh = MLSTM_HEADS
    dh = q_ref.shape[1] // nh
    j = pl.program_id(1)

    @pl.when(j == 0)
    def _():
        c_ref[...] = c0_ref[...]
        n_ref[...] = n0_ref[...]
        m_ref[...] = m0_ref[...]

    r, t, seq_c, seq_r, pos_c, pos_r = _row_ids(R, rps, R)
    valid_c = pos_c < l_valid
    valid_r = (t < R) & (pos_r < l_valid)
    same = (seq_c == seq_r) & (t < R)
    causal = same & (t <= r)

    pre_c = g_ref[...] + bifr_ref[...]
    pre_r = gt_ref[0] + bifc_ref[...]
    f_c = jnp.where(valid_c, _log_sigmoid(pre_c), 0.0)
    f_r = jnp.where(valid_r, _log_sigmoid(pre_r), 0.0)
    i_c = jnp.where(valid_c, pre_c, NEG)
    i_r = jnp.where(valid_r, pre_r, NEG)
    rq = lax.broadcasted_iota(jnp.int32, (1, R), 1)
    lower = (((r // rps) == (rq // rps)) & (rq <= r)).astype(BF16) if rps != R else (rq <= r).astype(BF16)
    b_c = _sel_dot(lower, f_c)
    tr = lax.broadcasted_iota(jnp.int32, (CHUNK, 1), 0)
    upper = ((tr <= t) & (tr < R) & ((tr // rps) == (t // rps))).astype(BF16)
    b_r = _dot_sel(f_r, upper)

    lane_h = lax.broadcasted_iota(jnp.int32, (1, nh), 1)
    m_old = [m_ref[s] for s in range(nseq)]
    m_new_rows = [jnp.zeros((1, nh), F32) for _ in range(nseq)]

    for h in range(nh):
        sl = slice(h * dh, (h + 1) * dh)
        q = q_ref[:, sl]
        ks = k_ref[:, sl] * jnp.asarray(dh ** -0.5, BF16)
        v = v_ref[:, sl]
        bc = b_c[:, nh + h:nh + h + 1]
        br = b_r[nh + h:nh + h + 1, :]
        ic = i_c[:, h:h + 1]
        ir = i_r[h:h + 1, :]
        log_d = jnp.where(causal, bc - br + ir, NEG)
        mm = _seq_select(seq_c, nseq, lambda s: m_old[s][:, h:h + 1])
        log_p = bc + mm
        m_t = jnp.maximum(log_p, jnp.max(log_d, axis=1, keepdims=True))
        d = jnp.exp(log_d - m_t)
        wp = jnp.exp(log_p - m_t)
        s_mat = _dot_nt(q, _pad_rows(ks, CHUNK)) * d
        qf = q.astype(F32)
        num_state = _seq_select(seq_c, nseq, lambda s: _dot(q, c_ref[0, s, h].astype(BF16)))
        den_state = _seq_select(
            seq_c, nseq, lambda s: jnp.sum(qf * n_ref[s, h:h + 1, :], axis=1, keepdims=True))
        num = _dot(s_mat.astype(BF16), _pad_rows(v, CHUNK)) + wp * num_state
        den = jnp.sum(s_mat, axis=1, keepdims=True) + wp * den_state
        hh = num / jnp.maximum(jnp.abs(den), jnp.exp(-m_t))
        hn = hh * lax.rsqrt(jnp.mean(hh * hh, axis=1, keepdims=True) + RMS_EPS) * gm_ref[:, sl]
        h_ref[:, sl] = (hn * _sigmoid(o_ref[:, sl].astype(F32))).astype(h_ref.dtype)

        last = [s * rps + rps - 1 if nseq > 1 else R - 1 for s in range(nseq)]
        b_end = _seq_select(seq_c, nseq, lambda s: bc[last[s]:last[s] + 1, :])
        m_end = _seq_select(seq_c, nseq, lambda s: m_t[last[s]:last[s] + 1, :])
        w_s = jnp.exp(b_end - bc + ic - m_end)
        kw = ks.astype(F32) * w_s
        for s in range(nseq):
            be = bc[last[s]:last[s] + 1, :]
            mn = m_t[last[s]:last[s] + 1, :]
            w_prev = jnp.exp(be + m_old[s][:, h:h + 1] - mn)
            kw_s = kw if nseq == 1 else jnp.where(seq_c == s, kw, 0.0)
            c_ref[0, s, h] = w_prev * c_ref[0, s, h] + _dot_tn(kw_s.astype(BF16), v)
            n_ref[s, h:h + 1, :] = w_prev * n_ref[s, h:h + 1, :] + jnp.sum(kw_s, axis=0, keepdims=True)
            m_new_rows[s] = m_new_rows[s] + jnp.where(lane_h == h, mn, 0.0)

    for s in range(nseq):
        m_ref[s] = m_new_rows[s]


def _drop_alias(body, n_in):
    def wrapped(*refs):
        return body(*refs[:n_in], *refs[n_in + 1:])
    return wrapped


def _layered_state_call(body, grid, in_specs, args, out_specs, out_shapes, big_out, state_in, state_blk,
                        layers, name, scratch_shapes=()):
    l_in, l_out, n_out_layers, prev_out = layers
    zeros = (0,) * (len(state_blk) - 1)
    in_specs = list(in_specs) + [pl.BlockSpec((1,) + state_blk, lambda b, j: (l_in, b) + zeros)]
    args = list(args) + [state_in]
    out_specs = list(out_specs)
    out_specs.insert(big_out, pl.BlockSpec((1,) + state_blk, lambda b, j: (l_out, b) + zeros))
    out_shapes = list(out_shapes)
    out_shapes.insert(big_out, jax.ShapeDtypeStruct((n_out_layers,) + state_in.shape[1:], F32))
    aliases = {}
    if prev_out is not None:
        body = _drop_alias(body, len(args))
        aliases = {len(args): big_out}
        in_specs.append(pl.BlockSpec(memory_space=pl.ANY))
        args.append(prev_out)
    return pl.pallas_call(
        body, grid=grid, in_specs=in_specs, out_specs=out_specs, out_shape=out_shapes,
        input_output_aliases=aliases, scratch_shapes=list(scratch_shapes),
        compiler_params=_cparams(("parallel", "arbitrary")), name=name)(*args)


def _mlstm(pab, gates, gates_t, bif_r, bif_c, g_m, n0, m0, c_all, layers, *, B, R, rps, nseq, nc, rb0,
           l_valid):
    nh = MLSTM_HEADS
    W = g_m.shape[1]
    dh = W // nh

    def rowmap(col):
        return lambda b, j: (rb0 + b * nc + j, col)

    def const(b, j):
        return (0, 0)

    nb = B // nseq
    body = functools.partial(_mlstm_body, R=R, rps=rps, nseq=nseq, l_valid=l_valid)
    st3 = pl.BlockSpec((nseq, nh, dh), lambda b, j: (b, 0, 0))
    stm = pl.BlockSpec((nseq, 1, nh), lambda b, j: (b, 0, 0))
    return _layered_state_call(
        body, (nb, nc),
        [pl.BlockSpec((R, W), rowmap(0)), pl.BlockSpec((R, W), rowmap(1)),
         pl.BlockSpec((R, W), rowmap(2)), pl.BlockSpec((R, W), rowmap(3)),
         pl.BlockSpec((R, LANES), rowmap(0)),
         pl.BlockSpec((1, LANES, CHUNK), lambda b, j: (b * nc + j, 0, 0)),
         pl.BlockSpec((1, LANES), const), pl.BlockSpec((LANES, 1), const),
         pl.BlockSpec((1, W), const), st3, stm],
        [pab, pab, pab, pab, gates, gates_t, bif_r, bif_c, g_m, n0, m0],
        [pl.BlockSpec((R, W), lambda b, j: (b * nc + j, 0)), st3, stm],
        [jax.ShapeDtypeStruct((nb * nc * R, W), BF16), jax.ShapeDtypeStruct(n0.shape, F32),
         jax.ShapeDtypeStruct(m0.shape, F32)],
        1, c_all, (nseq, nh, dh, dh), layers, "mlstm")


def _ssd_body(xbc_ref, z_ref, g_ref, gt_ref, prev_ref, cw_ref, cb_ref, dtbr_ref, dtbc_ref,
              alr_ref, alc_ref, dsk_ref, h0_ref, y_ref, hs_ref, prev_s, *, R, rps, nseq, l_valid, tile_rows):
    nheads = h0_ref.shape[2]
    P = SSD_HEADDIM
    N = SSD_STATE
    G = SSD_GROUPS
    hg = nheads // G
    W = nheads * P
    j = pl.program_id(1)

    @pl.when(j == 0)
    def _():
        hs_ref[...] = h0_ref[...]
        prev_s[...] = prev_ref[...].reshape(prev_s.shape)

    r, t, seq_c, seq_r, pos_c, pos_r = _row_ids(R, rps, R)
    valid_c = pos_c < l_valid
    valid_r = (t < R) & (pos_r < l_valid)
    causal = (seq_c == seq_r) & (t < R) & (t <= r)

    u = xbc_ref[...]
    taps = _conv_shift(u, prev_s[...], R, rps, tile_rows)
    if nseq == 1:
        prev_s[...] = u[R - tile_rows:, :].astype(F32)
    acc = cb_ref[...] + taps[0] * cw_ref[CONV_W - 1:CONV_W, :]
    for jj in range(1, CONV_W):
        acc = acc + taps[jj] * cw_ref[CONV_W - 1 - jj:CONV_W - jj, :]
    xc = acc * _sigmoid(acc)
    x = xc[:, :W]
    bm = xc[:, W:W + G * N]
    cm = xc[:, W + G * N:]

    off = 2 * MLSTM_HEADS
    dt_c = jnp.where(valid_c, _softplus(g_ref[...] + dtbr_ref[...]), 0.0)
    dt_r = jnp.where(valid_r, _softplus(gt_ref[0] + dtbc_ref[...]), 0.0)
    adt_c = dt_c * (-jnp.exp(alr_ref[...]))
    adt_r = dt_r * (-jnp.exp(alc_ref[...]))
    rq = lax.broadcasted_iota(jnp.int32, (1, R), 1)
    lower = (((r // rps) == (rq // rps)) & (rq <= r)).astype(BF16) if rps != R else (rq <= r).astype(BF16)
    ac_c = _sel_dot(lower, adt_c)
    tr = lax.broadcasted_iota(jnp.int32, (CHUNK, 1), 0)
    upper = ((tr <= t) & (tr < R) & ((tr // rps) == (t // rps))).astype(BF16)
    ac_r = _dot_sel(adt_r, upper)

    last = [s * rps + rps - 1 if nseq > 1 else R - 1 for s in range(nseq)]
    ac_end = _seq_select(seq_c, nseq, lambda s: ac_c[last[s]:last[s] + 1, :])
    eh = lax.broadcasted_iota(jnp.int32, (LANES, W), 0)
    el = lax.broadcasted_iota(jnp.int32, (LANES, W), 1)
    expand = (eh == off + el // P).astype(BF16)
    stacked = jnp.concatenate([dt_c, jnp.exp(ac_c), jnp.exp(ac_end - ac_c)], axis=0)
    full = _dot_sel(stacked, expand)
    dt_full, dec_full, tail_full = full[:R], full[R:2 * R], full[2 * R:]
    xdt = x * dt_full
    xdt_b = _pad_rows(xdt.astype(BF16), CHUNK)
    xw = (xdt * tail_full).astype(BF16)

    for g in range(G):
        cg = cm[:, g * N:(g + 1) * N].astype(BF16)
        bg = bm[:, g * N:(g + 1) * N].astype(BF16)
        cb = _dot_nt(cg, _pad_rows(bg, CHUNK))
        gsl = slice(g * hg * P, (g + 1) * hg * P)
        y_inter = _seq_select(
            seq_c, nseq,
            lambda s: _dot_nt(cg, hs_ref[0, s, g * hg:(g + 1) * hg].reshape(hg * P, N).astype(BF16)))
        ys = []
        for hh in range(hg):
            h = g * hg + hh
            seg = jnp.exp(jnp.where(causal, ac_c[:, off + h:off + h + 1] - ac_r[off + h:off + h + 1, :], NEG))
            ys.append(_dot((cb * seg).astype(BF16), xdt_b[:, h * P:(h + 1) * P]))
        y_g = jnp.concatenate(ys, axis=1) + y_inter * dec_full[:, gsl]
        y_g = y_g + dsk_ref[:, gsl] * x[:, gsl]
        zf = z_ref[:, gsl].astype(F32)
        y_ref[:, gsl] = (y_g * (zf * _sigmoid(zf))).astype(y_ref.dtype)
        for s in range(nseq):
            xw_s = xw[:, gsl] if nseq == 1 else jnp.where(seq_c == s, xw[:, gsl], jnp.zeros((), BF16))
            upd = _dot_tn(xw_s, bg)
            for hh in range(hg):
                h = g * hg + hh
                dec = jnp.exp(ac_c[last[s]:last[s] + 1, off + h:off + h + 1])
                hs_ref[0, s, h] = dec * hs_ref[0, s, h] + upd[hh * P:(hh + 1) * P, :]


def _ssd(pc, pab, gates, gates_t, prev, conv_w, conv_b, dtb_r, dtb_c, al_r, al_c, dskip_full, h_all, layers,
         *, B, R, rps, nseq, nc, rb0, l_valid, zcol):
    T, CW = pc.shape
    nheads = h_all.shape[2]
    W = nheads * SSD_HEADDIM
    tile_rows = prev.shape[1]
    nb = B // nseq

    def rowmap(col):
        return lambda b, j: (rb0 + b * nc + j, col)

    def const(b, j):
        return (0, 0)

    body = functools.partial(_ssd_body, R=R, rps=rps, nseq=nseq, l_valid=l_valid, tile_rows=tile_rows)
    return _layered_state_call(
        body, (nb, nc),
        [pl.BlockSpec((R, CW), rowmap(0)), pl.BlockSpec((R, W), rowmap(zcol)),
         pl.BlockSpec((R, LANES), rowmap(0)),
         pl.BlockSpec((1, LANES, CHUNK), lambda b, j: (b * nc + j, 0, 0)),
         pl.BlockSpec((nseq, tile_rows, CW), lambda b, j: (b, 0, 0)),
         pl.BlockSpec((CONV_W, CW), const), pl.BlockSpec((1, CW), const),
         pl.BlockSpec((1, LANES), const), pl.BlockSpec((LANES, 1), const),
         pl.BlockSpec((1, LANES), const), pl.BlockSpec((LANES, 1), const),
         pl.BlockSpec((1, W), const)],
        [pc, pab, gates, gates_t, prev, conv_w, conv_b, dtb_r, dtb_c, al_r, al_c, dskip_full],
        [pl.BlockSpec((R, W), lambda b, j: (b * nc + j, 0))],
        [jax.ShapeDtypeStruct((nb * nc * R, W), BF16)],
        1, h_all, (nseq, nheads, SSD_HEADDIM, SSD_STATE), layers, "ssd",
        scratch_shapes=[pltpu.VMEM((nseq * tile_rows, CW), F32)])


def _lru_body(xl_ref, yl_ref, prev_ref, h0_ref, cw_ref, cb_ref, wa_ref, ba_ref, wi_ref, bi_ref,
              lam_ref, o_ref, hs_ref, prev_s, *, R, rps, nseq, l_valid, tile_rows):
    W = xl_ref.shape[1]
    bw = W // LRU_BLOCKS
    j = pl.program_id(1)

    @pl.when(j == 0)
    def _():
        hs_ref[...] = h0_ref[...]
        prev_s[...] = prev_ref[...].reshape(prev_s.shape)

    r, t, seq_c, seq_r, pos_c, pos_r = _row_ids(R, rps, R)
    valid_c = pos_c < l_valid
    u = xl_ref[...]
    taps = _conv_shift(u, prev_s[...], R, rps, tile_rows)
    if nseq == 1:
        prev_s[...] = u[R - tile_rows:, :].astype(F32)
    xl = cb_ref[...] + taps[0] * cw_ref[CONV_W - 1:CONV_W, :]
    for jj in range(1, CONV_W):
        xl = xl + taps[jj] * cw_ref[CONV_W - 1 - jj:CONV_W - jj, :]
    xb = xl.astype(BF16)
    ra = jnp.concatenate([_dot(xb[:, n * bw:(n + 1) * bw], wa_ref[n]) for n in range(LRU_BLOCKS)], axis=1)
    ri = jnp.concatenate([_dot(xb[:, n * bw:(n + 1) * bw], wi_ref[n]) for n in range(LRU_BLOCKS)], axis=1)
    rg = _sigmoid(ra + ba_ref[...])
    ig = _sigmoid(ri + bi_ref[...])
    log_a = -LRU_C * rg * _softplus(-lam_ref[...])
    a = jnp.where(valid_c, jnp.exp(log_a), 1.0)
    th = jnp.tanh(log_a)
    bv = jnp.where(valid_c, jnp.sqrt(-2.0 * th / (1.0 - th)) * (ig * xl), 0.0)
    i_in = r % rps if rps != R else r
    step = 1
    while step < rps:
        a_sh = pltpu.roll(a, step, 0)
        b_sh = pltpu.roll(bv, step, 0)
        m = i_in >= step
        bv = jnp.where(m, a * b_sh + bv, bv)
        a = jnp.where(m, a * a_sh, a)
        step *= 2
    h0 = _seq_select(seq_c, nseq, lambda s: hs_ref[s])
    h_seq = a * h0 + bv
    for s in range(nseq):
        lr = s * rps + rps - 1 if nseq > 1 else R - 1
        hs_ref[s] = h_seq[lr:lr + 1, :]
    o_ref[...] = (h_seq * _gelu_tanh(yl_ref[...].astype(F32))).astype(o_ref.dtype)


def _lru(pab, prev, h0, conv_w, conv_b, wa, ba, wi, bi, lam, *, B, R, rps, nseq, nc, rb0, l_valid,
         xcol, ycol):
    T = pab.shape[0]
    W = lam.shape[1]
    bw = W // LRU_BLOCKS
    tile_rows = prev.shape[1]
    nb = B // nseq

    def rowmap(col):
        return lambda b, j: (rb0 + b * nc + j, col)

    def const(b, j):
        return (0, 0)

    def const3(b, j):
        return (0, 0, 0)

    body = functools.partial(_lru_body, R=R, rps=rps, nseq=nseq, l_valid=l_valid, tile_rows=tile_rows)
    st = pl.BlockSpec((nseq, 1, W), lambda b, j: (b, 0, 0))
    vec = pl.BlockSpec((1, W), const)
    return pl.pallas_call(
        body, grid=(nb, nc),
        in_specs=[pl.BlockSpec((R, W), rowmap(xcol)), pl.BlockSpec((R, W), rowmap(ycol)),
                  pl.BlockSpec((nseq, tile_rows, W), lambda b, j: (b, 0, 0)), st,
                  pl.BlockSpec((CONV_W, W), const), vec,
                  pl.BlockSpec((LRU_BLOCKS, bw, bw), const3), vec,
                  pl.BlockSpec((LRU_BLOCKS, bw, bw), const3), vec, vec],
        out_specs=[pl.BlockSpec((R, W), lambda b, j: (b * nc + j, 0)), st],
        out_shape=[jax.ShapeDtypeStruct((nb * nc * R, W), BF16), jax.ShapeDtypeStruct(h0.shape, F32)],
        scratch_shapes=[pltpu.VMEM((nseq * tile_rows, W), F32)],
        compiler_params=_cparams(("parallel", "arbitrary")), name="lru",
    )(pab, pab, prev, h0, conv_w, conv_b, wa, ba, wi, bi, lam)


def _outproj_body(hmp_ref, hsp_ref, hlp_ref, hms_ref, hss_ref, hls_ref, x_ref, wm_ref, ws_ref, wl_ref,
                  gs_ref, g_ref, b_ref, wr_ref, br_ref, x1_ref, x1b_ref, idx_ref, gate_ref, *, alpha, npb):
    is_p = pl.program_id(0) < npb
    hm = jnp.where(is_p, hmp_ref[...], hms_ref[...])
    hl = jnp.where(is_p, hlp_ref[...], hls_ref[...])
    ysf = jnp.where(is_p, hsp_ref[...], hss_ref[...]).astype(F32)
    ysn = ysf * lax.rsqrt(jnp.mean(ysf * ysf, axis=1, keepdims=True) + RMS_EPS) * gs_ref[...]
    y = _dot(hm, wm_ref[...]) + _dot(ysn.astype(BF16), ws_ref[...]) + _dot(hl, wl_ref[...])
    x1 = _layer_norm(alpha * x_ref[...] + y, g_ref[...], b_ref[...])
    x1b = x1.astype(BF16)
    x1_ref[...] = x1
    x1b_ref[...] = x1b
    logits = _dot(x1b, wr_ref[...]) + br_ref[...]
    lane = lax.broadcasted_iota(jnp.int32, logits.shape, 1)
    idx_out = jnp.zeros(logits.shape, jnp.int32)
    val_out = jnp.zeros(logits.shape, F32)
    top = None
    for k in range(TOP_K):
        mx = jnp.max(logits, axis=1, keepdims=True)
        sel = jnp.min(jnp.where(logits == mx, lane, LANES), axis=1, keepdims=True)
        if top is None:
            top = mx
        idx_out = jnp.where(lane == k, sel, idx_out)
        val_out = jnp.where(lane == k, jnp.exp(mx - top), val_out)
        logits = jnp.where(lane == sel, -jnp.inf, logits)
    gate_ref[...] = val_out / jnp.sum(val_out, axis=1, keepdims=True)
    idx_ref[...] = idx_out


def _outproj(mix_p, mix_s, x, wm, ws, wl, g_ssd, ln_g, ln_b, wr, br, alpha):
    T, D = x.shape
    tm = OUT_TILE
    npb = mix_p[0].shape[0] // tm
    assert npb * tm == mix_p[0].shape[0] and mix_s[0].shape[0] % tm == 0

    def rows(w):
        return pl.BlockSpec((tm, w), lambda i: (i, 0))

    def rows_p(a):
        return pl.BlockSpec((tm, a.shape[1]), lambda i: (jnp.minimum(i, npb - 1), 0))

    def rows_s(a):
        return pl.BlockSpec((tm, a.shape[1]), lambda i: (jnp.maximum(i - npb, 0), 0))

    def whole(a):
        return pl.BlockSpec(a.shape, lambda i: (0, 0), pipeline_mode=pl.Buffered(1))

    return pl.pallas_call(
        functools.partial(_outproj_body, alpha=alpha, npb=npb), grid=(T // tm,),
        in_specs=[rows_p(mix_p[0]), rows_p(mix_p[1]), rows_p(mix_p[2]),
                  rows_s(mix_s[0]), rows_s(mix_s[1]), rows_s(mix_s[2]), rows(D),
                  whole(wm), whole(ws), whole(wl), whole(g_ssd), whole(ln_g), whole(ln_b),
                  whole(wr), whole(br)],
        out_specs=[rows(D), rows(D), rows(LANES), rows(LANES)],
        out_shape=[jax.ShapeDtypeStruct((T, D), F32), jax.ShapeDtypeStruct((T, D), BF16),
                   jax.ShapeDtypeStruct((T, LANES), jnp.int32), jax.ShapeDtypeStruct((T, LANES), F32)],
        compiler_params=_cparams(("parallel",)), name="outproj_ln_router",
    )(*mix_p, *mix_s, x, wm, ws, wl, g_ssd, ln_g, ln_b, wr, br)


def _expert_body(ie_ref, ir0_ref, ins_ref, inv_ref, na_ref, tok_ref, dst_ref,
                 x_hbm, wg_ref, bg_ref, wu_ref, bu_ref, wd_ref, bd_ref, yg_in, yg_hbm,
                 xbuf, acc, outbuf, sem_in, sem_out, *, n_items, nf):
    del yg_in
    i = pl.program_id(0)
    j = pl.program_id(1)
    slot = i % 2
    half = wg_ref.shape[2] // 2

    def row_in(it_slot, tok, r):
        return pltpu.make_async_copy(x_hbm.at[pl.ds(tok, 1)], xbuf.at[it_slot, pl.ds(r, 1)], sem_in.at[it_slot])

    def row_out(r, dst):
        return pltpu.make_async_copy(outbuf.at[pl.ds(r, 1)], yg_hbm.at[pl.ds(dst, 1)], sem_out.at[0])

    def start_gather(it, it_slot):
        r0 = ir0_ref[it]

        def go(g, c):
            for u in range(DMA_UNROLL):
                r = g * DMA_UNROLL + u
                row_in(it_slot, tok_ref[r0 + r], r).start()
            return c
        lax.fori_loop(0, (inv_ref[it] + DMA_UNROLL - 1) // DMA_UNROLL, go, 0)

    def wait_gather(it, it_slot):
        def go(g, c):
            for u in range(DMA_UNROLL):
                row_in(it_slot, 0, 0).wait()
            return c
        lax.fori_loop(0, (inv_ref[it] + DMA_UNROLL - 1) // DMA_UNROLL, go, 0)

    @pl.when(j == 0)
    def _():
        @pl.when(i == 0)
        def _():
            xbuf[...] = jnp.zeros_like(xbuf)
            start_gather(0, 0)

        for static_slot in range(2):
            @pl.when(slot == static_slot)
            def _():
                wait_gather(i, static_slot)

                @pl.when(i + 1 < n_items)
                def _():
                    start_gather(i + 1, 1 - static_slot)

    r0 = ir0_ref[i]
    n_valid = inv_ref[i]
    n_sub = ins_ref[i]
    d_model = wd_ref.shape[3]

    def compute(m):
        wg = wg_ref[0, 0].astype(BF16)
        wu = wu_ref[0, 0].astype(BF16)
        wd = wd_ref[0, 0].astype(BF16)
        w = xbuf[slot, :m, :]
        lo = lax.bitcast_convert_type(w << 16, F32).astype(BF16)
        hi = lax.bitcast_convert_type(w & jnp.uint32(0xFFFF0000), F32).astype(BF16)
        gate = jnp.minimum(_dot(lo, wg[:half]) + _dot(hi, wg[half:]) + bg_ref[0, 0], SWIGLU_LIMIT)
        up = jnp.clip(_dot(lo, wu[:half]) + _dot(hi, wu[half:]) + bu_ref[0, 0], -SWIGLU_LIMIT, SWIGLU_LIMIT)
        act = ((up + 1.0) * gate * _sigmoid(SWIGLU_ALPHA * gate)).astype(BF16)

        def tile(n, add_acc, add_bias):
            cols = slice(n * MOE_TN, (n + 1) * MOE_TN)
            part = _dot(act, wd[:, cols])
            if add_acc:
                part = part + acc[:m, cols]
            if add_bias:
                part = part + bd_ref[0, 0, :, cols]
            return cols, part

        def down(add_acc, add_bias):
            for n in range(d_model // MOE_TN):
                cols, part = tile(n, add_acc, add_bias)
                acc[:m, cols] = part

        def bf16_bits(v):
            return lax.bitcast_convert_type(v.astype(BF16).astype(F32), jnp.uint32)

        def down_final(add_acc, add_bias):
            nh2 = d_model // MOE_TN // 2
            for n in range(nh2):
                cols, lo_part = tile(n, add_acc, add_bias)
                _, hi_part = tile(n + nh2, add_acc, add_bias)
                outbuf[:m, cols] = (bf16_bits(lo_part) >> 16) | bf16_bits(hi_part)

        if nf == 1:
            down_final(False, True)
        else:
            pl.when(j == 0)(lambda: down(False, True))
            pl.when((j > 0) & (j < nf - 1))(lambda: down(True, False))
            pl.when(j == nf - 1)(lambda: down_final(True, False))

    small = MOE_NSUB // 2 + 1

    @pl.when(j == nf - 1)
    def _():
        def wait8(g, c):
            for u in range(DMA_UNROLL):
                row_out(0, 0).wait()
            return c

        def wait1(q, c):
            row_out(0, 0).wait()
            return c
        n_prev = jnp.where(i > 0, inv_ref[jnp.maximum(i - 1, 0)], 0)
        lax.fori_loop(0, n_prev // DMA_UNROLL, wait8, 0)
        lax.fori_loop(n_prev // DMA_UNROLL * DMA_UNROLL, n_prev, wait1, 0)

    @pl.when((n_sub > 0) & (n_sub <= small))
    def _():
        compute(small * MOE_SUB)

    @pl.when(n_sub > small)
    def _():
        compute(MOE_NSUB * MOE_SUB)

    @pl.when(j == nf - 1)
    def _():
        def go8(g, c):
            for u in range(DMA_UNROLL):
                r = g * DMA_UNROLL + u
                row_out(r, dst_ref[r0 + r]).start()
            return c
        lax.fori_loop(0, n_valid // DMA_UNROLL, go8, 0)

        def go1(r, c):
            row_out(r, dst_ref[r0 + r]).start()
            return c
        lax.fori_loop(n_valid // DMA_UNROLL * DMA_UNROLL, n_valid, go1, 0)

        @pl.when(i == n_items - 1)
        def _():
            def wait1(q, c):
                row_out(0, 0).wait()
                return c
            lax.fori_loop(0, n_valid, wait1, 0)


def _experts(xw, items, tok, dst, wg, bg, wu, bu, wd, bd, yg, l):
    Fh = wg.shape[3]
    D = wg.shape[2]
    item_e, item_r0, item_ns, item_nv, n_act = items
    n_items = item_e.shape[0]
    nf = Fh // MOE_TF
    cap = MOE_NSUB * MOE_SUB

    def jeff(i, j, na):
        return jnp.where(i < na[0], j, nf - 1)

    def wmap(i, j, ie, ir0, ins, inv, na, tk, ds):
        return (l, ie[i], 0, jeff(i, j, na))

    def wdmap(i, j, ie, ir0, ins, inv, na, tk, ds):
        return (l, ie[i], jeff(i, j, na), 0)

    def bdmap(i, j, ie, ir0, ins, inv, na, tk, ds):
        return (l, ie[i], 0, 0)

    grid_spec = pltpu.PrefetchScalarGridSpec(
        num_scalar_prefetch=7, grid=(n_items, nf),
        in_specs=[pl.BlockSpec(memory_space=pl.ANY),
                  pl.BlockSpec((1, 1, D, MOE_TF), wmap), pl.BlockSpec((1, 1, 1, MOE_TF), wmap),
                  pl.BlockSpec((1, 1, D, MOE_TF), wmap), pl.BlockSpec((1, 1, 1, MOE_TF), wmap),
                  pl.BlockSpec((1, 1, MOE_TF, D), wdmap), pl.BlockSpec((1, 1, 1, D), bdmap),
                  pl.BlockSpec(memory_space=pl.ANY)],
        out_specs=pl.BlockSpec(memory_space=pl.ANY),
        scratch_shapes=[pltpu.VMEM((2, cap, D // 2), jnp.uint32), pltpu.VMEM((cap, D), F32),
                        pltpu.VMEM((cap, D // 2), jnp.uint32), pltpu.SemaphoreType.DMA((2,)),
                        pltpu.SemaphoreType.DMA((1,))])
    return pl.pallas_call(
        functools.partial(_expert_body, n_items=n_items, nf=nf), grid_spec=grid_spec,
        out_shape=jax.ShapeDtypeStruct(yg.shape, jnp.uint32), input_output_aliases={14: 0},
        compiler_params=_cparams(("arbitrary", "arbitrary")), name="moe_experts",
    )(item_e, item_r0, item_ns, item_nv, n_act, tok, dst, xw, wg, bg, wu, bu, wd, bd, yg)


def _combine_body(yg_ref, gate_ref, x_ref, g_ref, b_ref, o_ref, ob_ref, *, alpha):
    gates = gate_ref[...]
    f_lo = f_hi = None
    for k in range(TOP_K):
        w = yg_ref[k]
        lo = gates[:, k:k + 1] * lax.bitcast_convert_type(w << 16, F32)
        hi = gates[:, k:k + 1] * lax.bitcast_convert_type(w & jnp.uint32(0xFFFF0000), F32)
        f_lo = lo if f_lo is None else f_lo + lo
        f_hi = hi if f_hi is None else f_hi + hi
    f = jnp.concatenate([f_lo, f_hi], axis=1)
    x2 = _layer_norm(alpha * x_ref[...] + f, g_ref[...], b_ref[...])
    o_ref[...] = x2
    ob_ref[...] = x2.astype(BF16)


def _combine_ln(yg, gates, x, ln_g, ln_b, alpha):
    T, D = x.shape
    tm = OUT_TILE
    rows = pl.BlockSpec((tm, D), lambda i: (i, 0))
    vec = pl.BlockSpec((1, D), lambda i: (0, 0))
    return pl.pallas_call(
        functools.partial(_combine_body, alpha=alpha), grid=(T // tm,),
        in_specs=[pl.BlockSpec((TOP_K, tm, D // 2), lambda i: (0, i, 0)),
                  pl.BlockSpec((tm, LANES), lambda i: (i, 0)), rows, vec, vec],
        out_specs=[rows, rows],
        out_shape=[jax.ShapeDtypeStruct((T, D), F32), jax.ShapeDtypeStruct((T, D), BF16)],
        compiler_params=_cparams(("parallel",)), name="moe_combine_ln",
    )(yg, gates, x, ln_g, ln_b)


def _moe_dispatch(idx, valid_tok, n_experts, n_rows, n_items):
    T = idx.shape[0]
    cap = MOE_NSUB * MOE_SUB
    flat_e = jnp.where(valid_tok[:, None], idx[:, :TOP_K], n_experts).reshape(-1)
    order = jnp.argsort(flat_e)
    counts = jnp.bincount(flat_e, length=n_experts + 1)[:n_experts]
    nsub = (counts + MOE_SUB - 1) // MOE_SUB
    pad_end = jnp.cumsum(nsub * MOE_SUB)
    pad_start = pad_end - nsub * MOE_SUB
    start = jnp.cumsum(counts) - counts
    nit = (nsub + MOE_NSUB - 1) // MOE_NSUB
    it_end = jnp.cumsum(nit)
    it_start = it_end - nit
    n_act = it_end[-1]
    t = jnp.arange(n_items)
    e_t = jnp.minimum(jnp.sum(t[:, None] >= it_end[None, :], axis=1), n_experts - 1)
    k = t - it_start[e_t]
    active = t < n_act
    e_last = e_t[jnp.maximum(n_act - 1, 0)]
    item_e = jnp.where(active, e_t, e_last)
    item_r0 = jnp.where(active, pad_start[e_t] + k * cap, 0)
    item_ns = jnp.where(active, jnp.minimum(nsub[e_t] - k * MOE_NSUB, MOE_NSUB), 0)
    item_nv = jnp.where(active, jnp.clip(counts[e_t] - k * cap, 0, cap), 0)
    items = tuple(a.astype(jnp.int32) for a in (item_e, item_r0, item_ns, item_nv, n_act.reshape(1)))
    p = jnp.arange(n_rows)
    row_e = jnp.minimum(jnp.sum(p[:, None] >= pad_end[None, :], axis=1), n_experts - 1)
    off = p - pad_start[row_e]
    valid = (off < counts[row_e]) & (p < pad_end[-1])
    a = order[jnp.where(valid, start[row_e] + off, 0)]
    tok = jnp.where(valid, a // TOP_K, 0).astype(jnp.int32)
    dst = jnp.where(valid, (a % TOP_K) * T + a // TOP_K, -1).astype(jnp.int32)
    return items, tok, dst


def kernel(x_prompt, x_sample, state_mlstm_C, state_mlstm_n, state_mlstm_m, state_ssd_h, state_ssd_conv,
           state_lru_h, state_lru_conv, meta, ln_in_g, ln_in_b, w_in, b_if, conv_ssd_w, conv_ssd_b, dt_bias,
           a_log, d_skip, g_ssd, g_mlstm, conv_lru_w, conv_lru_b, w_rg_a, b_rg_a, w_rg_i, b_rg_i, lam, w_out,
           ln1_g, ln1_b, w_router, b_router, w_gate, b_gate, w_up, b_up, w_down, b_down, ln2_g, ln2_b):
    Bp, S, D = x_prompt.shape
    Bs, Ls, _ = x_sample.shape
    depth = w_in.shape[0]
    n_meta = meta.shape[0]
    nh = MLSTM_HEADS
    MW = g_mlstm.shape[1]
    SW = g_ssd.shape[1]
    LW = lam.shape[1]
    CW = conv_ssd_w.shape[2]
    n_ssd_heads = dt_bias.shape[1]
    E = w_router.shape[2]
    alpha = (2 * depth) ** 0.25
    Lp = n_meta + S
    LP = -(-Lp // CHUNK) * CHUNK
    LS = SAMPLE_LS
    Tp, Ts = Bp * LP, Bs * LS
    T = Tp + Ts
    ncp = LP // CHUNK
    RS = SAMPLE_SEQS * LS
    assert CONV_W - 1 <= Ls <= LS and T % ROW_TILE == 0 and T % OUT_TILE == 0 and Bs % SAMPLE_SEQS == 0 and Tp % RS == 0

    xp = jnp.concatenate([jnp.broadcast_to(meta.astype(F32)[None], (Bp, n_meta, D)), x_prompt,
                          jnp.zeros((Bp, LP - Lp, D), F32)], axis=1).reshape(Tp, D)
    xs = jnp.concatenate([x_sample, jnp.zeros((Bs, LS - Ls, D), F32)], axis=1).reshape(Ts, D)
    x, xb = _ln_in(jnp.concatenate([xp, xs], axis=0), ln_in_g, ln_in_b)
    pos = np.concatenate([np.tile(np.arange(LP) < Lp, Bp), np.tile(np.arange(LS) < Ls, Bs)])
    valid_tok = jnp.asarray(pos)
    tail_p = (np.arange(Bp)[:, None] * LP + np.arange(Lp - (CONV_W - 1), Lp)[None, :]).reshape(-1)
    tail_s = (Tp + np.arange(Bs)[:, None] * LS + np.arange(Ls - (CONV_W - 1), Ls)[None, :]).reshape(-1)

    sizes = [MW, MW, MW, MW, nh, nh, SW, CW, n_ssd_heads, LW, LW]
    o = [0] + [int(v) for v in np.cumsum(sizes)]
    w_ab = jnp.concatenate([w_in[:, :, o[0]:o[4]], w_in[:, :, o[6]:o[7]], w_in[:, :, o[9]:o[11]]],
                           axis=2).astype(BF16)
    w_c = w_in[:, :, o[7]:o[8]].astype(BF16)
    n_gate = 2 * nh + n_ssd_heads
    w_d = jnp.concatenate([w_in[:, :, o[4]:o[6]], w_in[:, :, o[8]:o[9]],
                           jnp.zeros((depth, D, LANES - n_gate), F32)], axis=2).astype(BF16)
    zcol = (4 * MW) // SW
    xcol = (4 * MW + SW) // LW
    ycol = xcol + 1
    assert zcol * SW == 4 * MW and xcol * LW == 4 * MW + SW

    def lane_pad(v, fill=0.0, front=0):
        return jnp.concatenate([jnp.full((front,), fill, F32), v.astype(F32),
                                jnp.full((LANES - front - v.shape[0],), fill, F32)])

    w_out_b = w_out.astype(BF16)
    wr_b = jnp.concatenate([w_router, jnp.zeros((depth, D, LANES - E), F32)], axis=2).astype(BF16)
    n_assign = (Bp * Lp + Bs * Ls) * TOP_K
    n_rows = -(-(n_assign + E * (MOE_SUB - 1)) // MOE_SUB) * MOE_SUB
    n_items = E + -(-(n_rows // MOE_SUB) // MOE_NSUB)
    yg = jnp.zeros((TOP_K * T, D // 2), jnp.uint32)

    zeros_p = dict(
        c=jnp.zeros((1, Bp, nh, MW // nh, MW // nh), F32), n=jnp.zeros((Bp, nh, MW // nh), F32),
        m=jnp.zeros((Bp, 1, nh), F32), h=jnp.zeros((1, Bp, n_ssd_heads, SSD_HEADDIM, SSD_STATE), F32),
        cs=jnp.zeros((Bp, 16, CW), F32), l=jnp.zeros((Bp, 1, LW), F32), cl=jnp.zeros((Bp, 16, LW), F32))
    pgrp = dict(B=Bp, R=CHUNK, rps=CHUNK, nseq=1, nc=ncp, rb0=0, l_valid=Lp)
    sgrp = dict(B=Bs, R=RS, rps=LS, nseq=SAMPLE_SEQS, nc=1, rb0=Tp // RS, l_valid=Ls)

    outs = [[] for _ in range(14)]
    bg4, bu4, bd4 = (b.reshape(depth, E, 1, b.shape[2]) for b in (b_gate, b_up, b_down))
    s_c_all = jnp.zeros(state_mlstm_C.shape, F32)
    s_h_all = jnp.zeros(state_ssd_h.shape, F32)
    for l in range(depth):
        pab = _matmul(xb, w_ab, l, BF16, 1024, "inproj_ab")
        pc = _matmul(xb, w_c, l, BF16, CW // 2, "inproj_c")
        gates = _matmul(xb, w_d, l, F32, LANES, "inproj_gates")
        gt_p = gates[:Tp].reshape(Bp * ncp, CHUNK, LANES).transpose(0, 2, 1)
        gt_s = gates[Tp:].reshape(Bs // SAMPLE_SEQS, RS, LANES).transpose(0, 2, 1)
        gt_s = jnp.concatenate([gt_s, jnp.zeros((Bs // SAMPLE_SEQS, LANES, CHUNK - RS), F32)], axis=2)

        bif_r = lane_pad(b_if[l]).reshape(1, LANES)
        bif_c = bif_r.reshape(LANES, 1)
        dtb_r = lane_pad(dt_bias[l], front=2 * nh).reshape(1, LANES)
        al_r = lane_pad(a_log[l], front=2 * nh).reshape(1, LANES)
        dsk = jnp.repeat(d_skip[l].astype(F32), SSD_HEADDIM).reshape(1, SW)
        gm = g_mlstm[l].reshape(1, MW).astype(F32)
        cw_s, cb_s = conv_ssd_w[l].astype(F32), conv_ssd_b[l].reshape(1, CW).astype(F32)
        cw_l, cb_l = conv_lru_w[l].astype(F32), conv_lru_b[l].reshape(1, LW).astype(F32)
        wa, wi = w_rg_a[l].astype(BF16), w_rg_i[l].astype(BF16)
        ba, bi = b_rg_a[l].reshape(1, LW).astype(F32), b_rg_i[l].reshape(1, LW).astype(F32)
        lam_l = lam[l].reshape(1, LW).astype(F32)

        def pad_conv(buf):
            return jnp.concatenate([jnp.zeros((buf.shape[0], SUBLANES - (CONV_W - 1), buf.shape[2]), F32),
                                    buf.astype(F32)], axis=1)

        st_s = dict(c=state_mlstm_C.astype(F32), n=state_mlstm_n[l].astype(F32),
                    m=state_mlstm_m[l].astype(F32).reshape(Bs, 1, nh), h=state_ssd_h.astype(F32),
                    cs=pad_conv(state_ssd_conv[l]), l=state_lru_h[l].astype(F32).reshape(Bs, 1, LW),
                    cl=pad_conv(state_lru_conv[l]))

        mixes, new_states = [], []
        for gi, (grp, st, gt) in enumerate(((pgrp, zeros_p, gt_p), (sgrp, st_s, gt_s))):
            lay_c = (0, 0, 1, None) if gi == 0 else (l, l, depth, s_c_all)
            lay_h = (0, 0, 1, None) if gi == 0 else (l, l, depth, s_h_all)
            hm_g, c_n, n_n, m_n = _mlstm(pab, gates, gt, bif_r, bif_c, gm, st["n"], st["m"], st["c"], lay_c, **grp)
            hs_g, h_n = _ssd(pc, pab, gates, gt, st["cs"], cw_s, cb_s, dtb_r, dtb_r.reshape(LANES, 1),
                             al_r, al_r.reshape(LANES, 1), dsk, st["h"], lay_h, zcol=zcol, **grp)
            if gi == 1:
                s_c_all, s_h_all = c_n, h_n
            hl_g, l_n = _lru(pab, st["cl"], st["l"], cw_l, cb_l, wa, ba, wi, bi, lam_l, xcol=xcol, ycol=ycol,
                             **grp)
            new_states.append((c_n, n_n, m_n, h_n, l_n))
            mixes.append((hm_g, hs_g, hl_g))

        x1, x1b, idx, gate_w = _outproj(
            mixes[0], mixes[1], x, w_out_b[l, :MW], w_out_b[l, MW:MW + SW], w_out_b[l, MW + SW:],
            g_ssd[l].reshape(1, SW).astype(F32), ln1_g[l].reshape(1, D), ln1_b[l].reshape(1, D),
            wr_b[l], lane_pad(b_router[l], fill=NEG).reshape(1, LANES), alpha)

        items, row_tok, row_dst = _moe_dispatch(idx, valid_tok, E, n_rows, n_items)
        bits = lax.bitcast_convert_type(x1b, jnp.uint16).astype(jnp.uint32)
        xw = bits[:, :D // 2] | (bits[:, D // 2:] << 16)
        yg = _experts(xw, items, row_tok, row_dst, w_gate, bg4, w_up, bu4, w_down, bd4, yg, l)
        x, xb = _combine_ln(yg.reshape(TOP_K, T, D // 2), gate_w, x1, ln2_g[l].reshape(1, D), ln2_b[l].reshape(1, D),
                            alpha)

        xbc_p = pc[tail_p].reshape(Bp, CONV_W - 1, CW).astype(F32)
        xbc_s = pc[tail_s].reshape(Bs, CONV_W - 1, CW).astype(F32)
        xl_p = pab[tail_p][:, xcol * LW:(xcol + 1) * LW].reshape(Bp, CONV_W - 1, LW).astype(F32)
        xl_s = pab[tail_s][:, xcol * LW:(xcol + 1) * LW].reshape(Bs, CONV_W - 1, LW).astype(F32)
        for gi, (bsz, conv_s, conv_l) in enumerate(((Bp, xbc_p, xl_p), (Bs, xbc_s, xl_s))):
            c_n, n_n, m_n, h_n, l_n = new_states[gi]
            vals = (c_n[0], n_n, m_n.reshape(bsz, nh), h_n[0], conv_s, l_n.reshape(bsz, LW), conv_l)
            for k, v in enumerate(vals):
                if gi == 0 or k not in (0, 3):
                    outs[gi * 7 + k].append(v)

    y_prompt = x[:Tp].reshape(Bp, LP, D)[:, n_meta:Lp]
    y_sample = x[Tp:].reshape(Bs, LS, D)[:, :Ls]
    state_dtypes = [state_mlstm_C.dtype, state_mlstm_n.dtype, state_mlstm_m.dtype, state_ssd_h.dtype,
                    state_ssd_conv.dtype, state_lru_h.dtype, state_lru_conv.dtype]
    stacked = [jnp.stack(v).astype(state_dtypes[k % 7]) if v else None for k, v in enumerate(outs)]
    stacked[7] = s_c_all.astype(state_dtypes[0])
    stacked[7 + 3] = s_h_all.astype(state_dtypes[3])
    return (y_prompt, y_sample, *stacked)
```
